```python
import math
import jax, jax.numpy as jnp
from jax import lax
import numpy as np

D_MODEL = 1024
BATCH = 16
SEQ = 2048
DEPTH = 2
DEC_BATCH = 128
DEC_SEQ = 8
PAST_LEN = 8192
PAGE_SIZE = 128

N_EVEN = (DEPTH + 1) // 2
N_ODD = DEPTH // 2
H_A = 8
DN_A = 64
DR_A = 32
DV_A = 64
D_CQ = 384
D_C = 256
ROPE_BASE = 10000.0
Q_BLOCK = 128
CHUNK = 128
G_B = 4
D_B = 512
C_WINDOWS = (128, 512, 2048)
C_DILATIONS = (1, 4, 16)
N_GROUPS_C = 3
H_C = 16
DH_C = 64
C_BLOCK = 128
N_BUCKETS = 32
MAX_DISTANCE = 2048
D_FF = 2816
N_EXPERTS = 8
TOP_K = 2
D_FF_E = 2816
ALPHA = (2 * DEPTH) ** 0.25
BETA = (8 * DEPTH) ** -0.25
LN_EPS = 1e-5
RMS_EPS = 1e-6
D_IN_E = D_CQ + D_C + DR_A + 2 * D_B
D_MIX_E = H_A * DV_A + D_B
D_IN_O = N_GROUPS_C * 3 * H_C * DH_C
D_MIX_O = H_C * DH_C

kernel_name = 'hybrid_mla_gmlp_dilated_moe_step'


def _layer_norm(x, g, b):
    xf = x.astype(jnp.float32)
    mu = jnp.mean(xf, -1, keepdims=True)
    var = jnp.mean(jnp.square(xf - mu), -1, keepdims=True)
    return ((xf - mu) * lax.rsqrt(var + LN_EPS) * g.astype(jnp.float32) + b.astype(jnp.float32)).astype(x.dtype)


def _rms_norm(x, g):
    xf = x.astype(jnp.float32)
    return (xf * lax.rsqrt(jnp.mean(jnp.square(xf), -1, keepdims=True) + RMS_EPS) * g.astype(jnp.float32)).astype(x.dtype)


def _post_ln(x, sub, g, b):
    return _layer_norm(ALPHA * x + sub, g, b)


def _rope(x, pos):
    half = DR_A // 2
    inv = ROPE_BASE ** (-jnp.arange(half, dtype=jnp.float32) / half)
    ang = pos.astype(jnp.float32)[:, None] * inv[None, :]
    cos = jnp.cos(ang)[None, :, None, :]
    sin = jnp.sin(ang)[None, :, None, :]
    xf = x.astype(jnp.float32)
    x1, x2 = xf[..., :half], xf[..., half:]
    return jnp.concatenate([x1 * cos - x2 * sin, x1 * sin + x2 * cos], -1).astype(x.dtype)


def _even_inputs(x, pos, w_in, q_norm_g, kv_norm_g, w_uq, v_ln_g, v_ln_b):
    h = jnp.einsum('bsd,de->bse', x, w_in)
    c_q, c_kv, k_r, uv = jnp.split(h, [D_CQ, D_CQ + D_C, D_CQ + D_C + DR_A], axis=-1)
    q = jnp.einsum('bsc,chn->bshn', _rms_norm(c_q, q_norm_g), w_uq)
    q_nope = q[..., :DN_A]
    q_rope = _rope(q[..., DN_A:], pos)
    c_kv = _rms_norm(c_kv, kv_norm_g)
    k_pe = _rope(k_r[:, :, None, :], pos)[:, :, 0, :]
    u, v = jnp.split(jax.nn.gelu(uv), 2, axis=-1)
    v = _layer_norm(v, v_ln_g, v_ln_b)
    return q_nope, q_rope, c_kv, k_pe, u, v


def _chunk_gate(u, v, w_s, b_s):
    nb, t, _ = v.shape
    L = min(t, CHUNK)
    n = t // L
    w = jnp.where(jnp.tril(jnp.ones((L, L), dtype=bool)), w_s[:, :L, :L], 0)
    vc = v.reshape(nb, n, L, G_B, D_B // G_B)
    f = jnp.einsum('gts,bnsgc->bntgc', w, vc) + jnp.transpose(b_s[:, :L])[None, None, :, :, None]
    return u * f.reshape(nb, t, D_B)


def _mla_prompt(q_nope, q_rope, c_kv, k_pe, w_uk, w_uv):
    nb, s = c_kv.shape[:2]
    k_nope = jnp.einsum('bsc,chn->bshn', c_kv, w_uk)
    v = jnp.einsum('bsc,chv->bshv', c_kv, w_uv)
    n_blk = s // Q_BLOCK
    qn = jnp.moveaxis(q_nope.reshape(nb, n_blk, Q_BLOCK, H_A, DN_A), 1, 0)
    qr = jnp.moveaxis(q_rope.reshape(nb, n_blk, Q_BLOCK, H_A, DR_A), 1, 0)
    k_pos = jnp.arange(s)
    scale = (DN_A + DR_A) ** -0.5

    def block(args):
        i, qn_b, qr_b = args
        logits = (jnp.einsum('bqhn,bkhn->bhqk', qn_b, k_nope)
                  + jnp.einsum('bqhr,bkr->bhqk', qr_b, k_pe)).astype(jnp.float32) * scale
        q_pos = i * Q_BLOCK + jnp.arange(Q_BLOCK)
        logits = jnp.where(k_pos[None, :] <= q_pos[:, None], logits, -jnp.inf)
        p = jax.nn.softmax(logits, axis=-1).astype(v.dtype)
        return jnp.einsum('bhqk,bkhv->bqhv', p, v)

    o = lax.map(block, (jnp.arange(n_blk), qn, qr))
    return jnp.moveaxis(o, 0, 1).reshape(nb, s, H_A * DV_A)


def _mla_sample(q_nope, q_rope, c_kv, k_pe, past_ckv_pages, past_kpe_pages, w_uk, w_uv):
    nb, t = c_kv.shape[:2]
    past_ckv = past_ckv_pages.reshape(nb, -1, D_C)
    past_kpe = past_kpe_pages.reshape(nb, -1, DR_A)
    n_past = past_ckv.shape[1]
    scale = (DN_A + DR_A) ** -0.5
    q_lat = jnp.einsum('bthn,chn->bthc', q_nope, w_uk)

    def logits(ckv, kpe):
        return (jnp.einsum('bthc,bsc->bhts', q_lat, ckv)
                + jnp.einsum('bthr,bsr->bhts', q_rope, kpe)).astype(jnp.float32) * scale

    l_new = jnp.where(jnp.tril(jnp.ones((t, t), dtype=bool)), logits(c_kv, k_pe), -jnp.inf)
    p = jax.nn.softmax(jnp.concatenate([logits(past_ckv, past_kpe), l_new], -1), axis=-1).astype(c_kv.dtype)
    o_lat = (jnp.einsum('bhts,bsc->bthc', p[..., :n_past], past_ckv)
             + jnp.einsum('bhts,bsc->bthc', p[..., n_past:], c_kv))
    return jnp.einsum('bthc,chv->bthv', o_lat, w_uv).reshape(nb, t, H_A * DV_A)


def _t5_bucket(dist):
    max_exact = N_BUCKETS // 2
    log_ratio = jnp.log(jnp.maximum(dist, 1).astype(jnp.float32) / max_exact) / math.log(MAX_DISTANCE / max_exact)
    large = jnp.minimum(max_exact + (log_ratio * (N_BUCKETS - max_exact)).astype(jnp.int32), N_BUCKETS - 1)
    return jnp.where(dist < max_exact, dist, large)


def _group_bias(rel_bias, g):
    d = C_DILATIONS[g]
    dist = d * jnp.arange(C_WINDOWS[g] // d + 1, dtype=jnp.int32)
    return jnp.transpose(rel_bias[_t5_bucket(dist), g * H_C:(g + 1) * H_C])


def _softmax_stats(logits, v, eq):
    m = jnp.max(logits, -1, keepdims=True)
    p = jnp.exp(logits - m)
    s = jnp.sum(p, -1, keepdims=True)
    o = jnp.einsum(eq, (p / s).astype(v.dtype), v)
    return o, (m + jnp.log(s))[..., 0]


def _dsw_prompt_group(q, k, v, bias, d):
    nb, s = q.shape[:2]
    L = s // d
    n_blk = -(-L // C_BLOCK)
    Lp = n_blk * C_BLOCK
    n_back = bias.shape[1] - 1

    def strided(x):
        x = jnp.transpose(x.reshape(nb, L, d, H_C, DH_C), (0, 2, 1, 3, 4))
        x = jnp.pad(x, ((0, 0), (0, 0), (0, Lp - L), (0, 0), (0, 0)))
        return x.reshape(nb, d, n_blk, C_BLOCK, H_C, DH_C)

    def with_prev(x):
        prev = jnp.pad(x, ((0, 0), (0, 0), (1, 0), (0, 0), (0, 0), (0, 0)))[:, :, :-1]
        return jnp.concatenate([prev, x], axis=3)

    qs = strided(q)
    kb = with_prev(strided(k))
    vb = with_prev(strided(v))
    a = jnp.arange(C_BLOCK)[:, None]
    c = jnp.arange(2 * C_BLOCK)[None, :]
    j = a + C_BLOCK - c
    band = (j >= 0) & (j <= n_back)
    has_prev = (jnp.arange(n_blk)[:, None, None] > 0) | (c[None] >= C_BLOCK)
    mask = (band[None] & has_prev)[None, None, :, None]
    logits = (jnp.einsum('brnqhe,brnkhe->brnhqk', qs, kb).astype(jnp.float32) * DH_C ** -0.5
              + bias[:, jnp.clip(j, 0, n_back)].astype(jnp.float32))
    logits = jnp.where(mask, logits, -jnp.inf)
    o, lse = _softmax_stats(logits, vb, 'brnhqk,brnkhe->brnqhe')
    o = jnp.transpose(o.reshape(nb, d, Lp, H_C, DH_C)[:, :, :L], (0, 2, 1, 3, 4)).reshape(nb, s, H_C, DH_C)
    lse = jnp.transpose(lse, (0, 1, 2, 4, 3)).reshape(nb, d, Lp, H_C)[:, :, :L]
    lse = jnp.transpose(lse, (0, 2, 1, 3)).reshape(nb, s, H_C)
    return o, lse


def _dsw_sample_group(q, k, v, buf_k, buf_v, bias, d):
    t = q.shape[1]
    n_buf = buf_k.shape[1]
    kk = jnp.concatenate([buf_k, k], axis=1)
    vv = jnp.concatenate([buf_v, v], axis=1)
    idx = n_buf + jnp.arange(t)[:, None] - d * jnp.arange(bias.shape[1])[None, :]
    valid = idx >= 0
    idx = jnp.maximum(idx, 0)
    logits = (jnp.einsum('bthe,btjhe->bthj', q, kk[:, idx]).astype(jnp.float32) * DH_C ** -0.5
              + bias.astype(jnp.float32))
    logits = jnp.where(valid[None, :, None, :], logits, -jnp.inf)
    return _softmax_stats(logits, vv[:, idx], 'bthj,btjhe->bthe')


def _combine_groups(outs, lses):
    w = jax.nn.softmax(jnp.stack(lses, 0), axis=0)
    o = jnp.sum(w[..., None] * jnp.stack(outs, 0).astype(jnp.float32), axis=0)
    return o.astype(outs[0].dtype)


def _dsw_mixer(x, w_in, w_out, biases, bufs, layer_idx):
    nb, s = x.shape[:2]
    qkv = jnp.einsum('bsd,de->bse', x, w_in).reshape(nb, s, N_GROUPS_C, 3, H_C, DH_C)
    outs, lses, new_rows = [], [], []
    for g in range(N_GROUPS_C):
        q, k, v = qkv[:, :, g, 0], qkv[:, :, g, 1], qkv[:, :, g, 2]
        if bufs is None:
            o, lse = _dsw_prompt_group(q, k, v, biases[g], C_DILATIONS[g])
            new_rows.append(jnp.stack([k, v], axis=2)[:, s - min(C_WINDOWS[g], s):])
        else:
            o, lse = _dsw_sample_group(q, k, v, bufs[g][layer_idx, :, :, 0], bufs[g][layer_idx, :, :, 1],
                                       biases[g], C_DILATIONS[g])
            new_rows.append(jnp.stack([k, v], axis=2))
        outs.append(o)
        lses.append(lse)
    y = jnp.einsum('bse,ed->bsd', _combine_groups(outs, lses).reshape(nb, s, D_MIX_O), w_out)
    return y, new_rows


def _swiglu(x, w_gate, w_up, w_down):
    h = jax.nn.silu(jnp.einsum('bsd,df->bsf', x, w_gate)) * jnp.einsum('bsd,df->bsf', x, w_up)
    return jnp.einsum('bsf,fd->bsd', h, w_down)


def _moe(x, w_router, w_gate, w_up, w_down):
    logits = jnp.einsum('bsd,de->bse', x, w_router).astype(jnp.float32)
    top_val, top_idx = lax.top_k(logits, TOP_K)
    gates = jax.nn.softmax(top_val, axis=-1)
    combine = jnp.sum(jax.nn.one_hot(top_idx, N_EXPERTS, dtype=jnp.float32) * gates[..., None], axis=-2)
    y = jnp.zeros(x.shape, jnp.float32)
    for e in range(N_EXPERTS):
        y = y + combine[..., e:e + 1] * _swiglu(x, w_gate[e], w_up[e], w_down[e]).astype(jnp.float32)
    return y.astype(x.dtype)


def setup_inputs(seed: int = 0) -> dict:
    key = jax.random.key(seed)
    keys = iter(jax.random.split(key, 40))

    def nrm(shape, scale=1.0):
        return jax.random.normal(next(keys), shape, jnp.float32) * scale

    def gain(shape):
        return 1.0 + nrm(shape, 0.02)

    n_pages = PAST_LEN // PAGE_SIZE
    n_used = DEC_BATCH * n_pages
    n_pool = n_used + max(1, n_used // 4)
    page_table = jax.random.permutation(next(keys), n_pool)[:n_used].reshape(DEC_BATCH, n_pages).astype(jnp.int32)

    inp = {}
    inp['x_prompt'] = nrm((BATCH, SEQ, D_MODEL))
    inp['x_sample'] = nrm((DEC_BATCH, DEC_SEQ, D_MODEL))
    inp['cache_mla_ckv'] = nrm((N_EVEN, n_pool, PAGE_SIZE, D_C))
    inp['cache_mla_kpe'] = nrm((N_EVEN, n_pool, PAGE_SIZE, DR_A))
    for g in range(N_GROUPS_C):
        inp[f'cache_dsw_kv{g}'] = nrm((N_ODD, DEC_BATCH, min(C_WINDOWS[g], PAST_LEN), 2, H_C, DH_C))
    inp['page_table'] = page_table
    inp['ln_g'] = gain((DEPTH, 2, D_MODEL))
    inp['ln_b'] = nrm((DEPTH, 2, D_MODEL), 0.02)
    inp['e_w_in'] = nrm((N_EVEN, D_MODEL, D_IN_E), D_MODEL ** -0.5)
    inp['e_q_norm_g'] = gain((N_EVEN, D_CQ))
    inp['e_kv_norm_g'] = gain((N_EVEN, D_C))
    inp['e_w_uq'] = nrm((N_EVEN, D_CQ, H_A, DN_A + DR_A), D_CQ ** -0.5)
    inp['e_w_uk'] = nrm((N_EVEN, D_C, H_A, DN_A), D_C ** -0.5)
    inp['e_w_uv'] = nrm((N_EVEN, D_C, H_A, DV_A), D_C ** -0.5)
    inp['e_v_ln_g'] = gain((N_EVEN, D_B))
    inp['e_v_ln_b'] = nrm((N_EVEN, D_B), 0.02)
    inp['e_w_s'] = nrm((N_EVEN, G_B, CHUNK, CHUNK), CHUNK ** -0.5)
    inp['e_b_s'] = gain((N_EVEN, G_B, CHUNK))
    inp['e_w_out'] = nrm((N_EVEN, D_MIX_E, D_MODEL), BETA * D_MIX_E ** -0.5)
    inp['ffn_w_gate'] = nrm((N_EVEN, D_MODEL, D_FF), D_MODEL ** -0.5)
    inp['ffn_w_up'] = nrm((N_EVEN, D_MODEL, D_FF), D_MODEL ** -0.5)
    inp['ffn_w_down'] = nrm((N_EVEN, D_FF, D_MODEL), BETA * D_FF ** -0.5)
    inp['o_w_in'] = nrm((N_ODD, D_MODEL, D_IN_O), D_MODEL ** -0.5)
    inp['o_w_out'] = nrm((N_ODD, D_MIX_O, D_MODEL), BETA * D_MIX_O ** -0.5)
    inp['rel_bias'] = nrm((N_BUCKETS, N_GROUPS_C * H_C), 0.2)
    inp['moe_w_router'] = nrm((N_ODD, D_MODEL, N_EXPERTS), D_MODEL ** -0.5)
    inp['moe_w_gate'] = nrm((N_ODD, N_EXPERTS, D_MODEL, D_FF_E), D_MODEL ** -0.5)
    inp['moe_w_up'] = nrm((N_ODD, N_EXPERTS, D_MODEL, D_FF_E), D_MODEL ** -0.5)
    inp['moe_w_down'] = nrm((N_ODD, N_EXPERTS, D_FF_E, D_MODEL), BETA * D_FF_E ** -0.5)
    return inp


def reference(x_prompt, x_sample, cache_mla_ckv, cache_mla_kpe, cache_dsw_kv0, cache_dsw_kv1, cache_dsw_kv2,
              page_table, ln_g, ln_b, e_w_in, e_q_norm_g, e_kv_norm_g, e_w_uq, e_w_uk, e_w_uv, e_v_ln_g, e_v_ln_b,
              e_w_s, e_b_s, e_w_out, ffn_w_gate, ffn_w_up, ffn_w_down, o_w_in, o_w_out, rel_bias,
              moe_w_router, moe_w_gate, moe_w_up, moe_w_down):
    nb_p, s = x_prompt.shape[:2]
    t = x_sample.shape[1]
    pos_p = jnp.arange(s, dtype=jnp.int32)
    pos_s = PAST_LEN + jnp.arange(t, dtype=jnp.int32)
    biases = [_group_bias(rel_bias, g) for g in range(N_GROUPS_C)]
    dsw_bufs = (cache_dsw_kv0, cache_dsw_kv1, cache_dsw_kv2)
    xp, xs = x_prompt, x_sample
    ckv_p, kpe_p, ckv_s, kpe_s, gv_s, dsw_p, dsw_s = [], [], [], [], [], [], []
    for layer in range(DEPTH):
        i = layer // 2
        if layer % 2 == 0:
            qn, qr, ckv, kpe, u, v = _even_inputs(xp, pos_p, e_w_in[i], e_q_norm_g[i], e_kv_norm_g[i], e_w_uq[i],
                                                  e_v_ln_g[i], e_v_ln_b[i])
            mix = jnp.concatenate([_mla_prompt(qn, qr, ckv, kpe, e_w_uk[i], e_w_uv[i]),
                                   _chunk_gate(u, v, e_w_s[i], e_b_s[i])], axis=-1)
            mix_p = jnp.einsum('bse,ed->bsd', mix, e_w_out[i])
            ckv_p.append(ckv.reshape(nb_p, s // PAGE_SIZE, PAGE_SIZE, D_C))
            kpe_p.append(kpe.reshape(nb_p, s // PAGE_SIZE, PAGE_SIZE, DR_A))
            qn, qr, ckv, kpe, u, v = _even_inputs(xs, pos_s, e_w_in[i], e_q_norm_g[i], e_kv_norm_g[i], e_w_uq[i],
                                                  e_v_ln_g[i], e_v_ln_b[i])
            att = _mla_sample(qn, qr, ckv, kpe, cache_mla_ckv[i, page_table], cache_mla_kpe[i, page_table],
                              e_w_uk[i], e_w_uv[i])
            mix = jnp.concatenate([att, _chunk_gate(u, v, e_w_s[i], e_b_s[i])], axis=-1)
            mix_s = jnp.einsum('bse,ed->bsd', mix, e_w_out[i])
            ckv_s.append(ckv)
            kpe_s.append(kpe)
            gv_s.append(v)
            xp = _post_ln(xp, mix_p, ln_g[layer, 0], ln_b[layer, 0])
            xs = _post_ln(xs, mix_s, ln_g[layer, 0], ln_b[layer, 0])
            xp = _post_ln(xp, _swiglu(xp, ffn_w_gate[i], ffn_w_up[i], ffn_w_down[i]), ln_g[layer, 1], ln_b[layer, 1])
            xs = _post_ln(xs, _swiglu(xs, ffn_w_gate[i], ffn_w_up[i], ffn_w_down[i]), ln_g[layer, 1], ln_b[layer, 1])
        else:
            mix_p, rows_p = _dsw_mixer(xp, o_w_in[i], o_w_out[i], biases, None, i)
            mix_s, rows_s = _dsw_mixer(xs, o_w_in[i], o_w_out[i], biases, dsw_bufs, i)
            dsw_p.append(rows_p)
            dsw_s.append(rows_s)
            xp = _post_ln(xp, mix_p, ln_g[layer, 0], ln_b[layer, 0])
            xs = _post_ln(xs, mix_s, ln_g[layer, 0], ln_b[layer, 0])
            xp = _post_ln(xp, _moe(xp, moe_w_router[i], moe_w_gate[i], moe_w_up[i], moe_w_down[i]),
                          ln_g[layer, 1], ln_b[layer, 1])
            xs = _post_ln(xs, _moe(xs, moe_w_router[i], moe_w_gate[i], moe_w_up[i], moe_w_down[i]),
                          ln_g[layer, 1], ln_b[layer, 1])
    new_ckv_prompt = jnp.stack(ckv_p)
    new_kpe_prompt = jnp.stack(kpe_p)
    new_ckv_sample = jnp.stack(ckv_s)
    new_kpe_sample = jnp.stack(kpe_s)
    new_gate_v_sample = jnp.stack(gv_s)
    new_dsw0_prompt = jnp.stack([r[0] for r in dsw_p])
    new_dsw1_prompt = jnp.stack([r[1] for r in dsw_p])
    new_dsw2_prompt = jnp.stack([r[2] for r in dsw_p])
    new_dsw0_sample = jnp.stack([r[0] for r in dsw_s])
    new_dsw1_sample = jnp.stack([r[1] for r in dsw_s])
    new_dsw2_sample = jnp.stack([r[2] for r in dsw_s])
    return (xp, xs, new_ckv_prompt, new_kpe_prompt, new_ckv_sample, new_kpe_sample, new_gate_v_sample,
            new_dsw0_prompt, new_dsw1_prompt, new_dsw2_prompt, new_dsw0_sample, new_dsw1_sample, new_dsw2_sample)
```

```python
import functools
import math

import numpy as np
import jax
import jax.numpy as jnp
from jax import lax
from jax.experimental import pallas as pl
from jax.experimental.pallas import tpu as pltpu

BF16 = jnp.bfloat16
F32 = jnp.float32

H_A, DN_A, DR_A, DV_A = 8, 64, 32, 64
D_CQ, D_C = 384, 256
ROPE_BASE = 10000.0
CHUNK, G_B, D_B = 128, 4, 512
C_WINDOWS = (128, 512, 2048)
C_DILATIONS = (1, 4, 16)
H_C, DH_C = 16, 64
C_BLOCK = 128
N_BUCKETS, MAX_DISTANCE = 32, 2048
N_EXPERTS, TOP_K = 8, 2
DEPTH = 2
ALPHA = (2 * DEPTH) ** 0.25
LN_EPS = 1e-5
RMS_EPS = 1e-6
PAGE_SIZE = 128
NEG = -1e30

LANES = 128
V7X_VMEM_BYTES = 64 * 2**20
VMEM_LIMIT = 56 * 2**20


def _dot(a, b):
    return jnp.dot(a, b, preferred_element_type=F32)


def _dot_nt(a, b):
    return lax.dot_general(a, b, (((1,), (1,)), ((), ())), preferred_element_type=F32)


def _params(n_axes, vmem=VMEM_LIMIT):
    return pltpu.CompilerParams(dimension_semantics=("arbitrary",) * n_axes, vmem_limit_bytes=vmem)


def _layer_norm(x, g, b):
    mu = jnp.mean(x, -1, keepdims=True)
    xc = x - mu
    var = jnp.mean(xc * xc, -1, keepdims=True)
    return xc * lax.rsqrt(var + LN_EPS) * g + b


def _rms_norm(x, g):
    return x * lax.rsqrt(jnp.mean(x * x, -1, keepdims=True) + RMS_EPS) * g


def _rope_apply(x, c, s1, s2):
    return x * c + pltpu.roll(x, LANES - 16, 1) * s1 + pltpu.roll(x, 16, 1) * s2


def _even_in_kernel(x_ref, w_in_ref, gq_ref, gkv_ref, w_uq_ref, w_kv_ref, vg_ref, vb_ref, tab_ref,
                    wgate_ref, bgate_ref,
                    q_ref, k_ref, v_ref, ckv_ref, kpe_ref, gated_ref, vln_ref):
    tm = x_ref.shape[0]
    h = _dot(x_ref[...].astype(BF16), w_in_ref[...])
    c_q = h[:, :D_CQ]
    c_kv = h[:, D_CQ:D_CQ + D_C]
    o_u = D_CQ + D_C
    u = jax.nn.gelu(h[:, o_u:o_u + D_B])
    vv = jax.nn.gelu(h[:, o_u + D_B:o_u + 2 * D_B])
    kr = h[:, o_u + 2 * D_B:]

    tab = tab_ref[...]
    cq, s1q, s2q = tab[:, 0:128], tab[:, 128:256], tab[:, 256:384]
    ck, s1k, s2k = tab[:, 384:512], tab[:, 512:640], tab[:, 640:768]

    q = _dot(_rms_norm(c_q, gq_ref[...]).astype(BF16), w_uq_ref[...])
    for hb in range(H_A):
        sl = slice(hb * LANES, (hb + 1) * LANES)
        q_ref[:, sl] = _rope_apply(q[:, sl], cq, s1q, s2q).astype(BF16)

    ckv = _rms_norm(c_kv, gkv_ref[...])
    ckv_ref[...] = ckv
    kpe = _rope_apply(kr, ck, s1k, s2k)
    kpe_ref[...] = kpe
    kv = _dot(jnp.concatenate([ckv.astype(BF16), kpe.astype(BF16)], axis=1), w_kv_ref[...])
    k_ref[...] = kv[:, :H_A * LANES].astype(BF16)
    v_ref[...] = kv[:, H_A * LANES:].astype(BF16)

    vln = _layer_norm(vv, vg_ref[...], vb_ref[...])
    vln_ref[...] = vln
    vb16 = vln.astype(BF16)
    gw = D_B // G_B
    for c in range(tm // CHUNK):
        rs = slice(c * CHUNK, (c + 1) * CHUNK)
        for g in range(G_B):
            cs = slice(g * gw, (g + 1) * gw)
            f = _dot(wgate_ref[g], vb16[rs, cs]) + bgate_ref[g]
            gated_ref[rs, cs] = (u[rs, cs] * f).astype(BF16)


def _even_in(x, prep, tab, tm, n_prompt_tiles, tab_blocks):
    m = x.shape[0]
    d_model = x.shape[1]
    n_in = prep["w_in"].shape[1]

    def full(a):
        return pl.BlockSpec(a.shape, lambda i: (0,) * a.ndim)

    def tab_map(i):
        return (jnp.where(i < n_prompt_tiles, i % tab_blocks, tab_blocks), 0)

    def gate_map(i):
        return (jnp.where(i < n_prompt_tiles, 0, 1), 0, 0, 0)

    row = lambda w: pl.BlockSpec((tm, w), lambda i: (i, 0))
    outs = [
        jax.ShapeDtypeStruct((m, H_A * LANES), BF16),
        jax.ShapeDtypeStruct((m, H_A * LANES), BF16),
        jax.ShapeDtypeStruct((m, H_A * DV_A), BF16),
        jax.ShapeDtypeStruct((m, D_C), F32),
        jax.ShapeDtypeStruct((m, LANES), F32),
        jax.ShapeDtypeStruct((m, D_B), BF16),
        jax.ShapeDtypeStruct((m, D_B), F32),
    ]
    return pl.pallas_call(
        _even_in_kernel,
        grid=(m // tm,),
        in_specs=[row(d_model), full(prep["w_in"]), full(prep["gq"]), full(prep["gkv"]), full(prep["w_uq"]),
                  full(prep["w_kv"]), full(prep["vg"]), full(prep["vb"]),
                  pl.BlockSpec((tm, 6 * LANES), tab_map),
                  pl.BlockSpec((None, G_B, CHUNK, CHUNK), gate_map),
                  pl.BlockSpec((None, G_B, CHUNK, LANES), gate_map)],
        out_specs=[row(o.shape[1]) for o in outs],
        out_shape=outs,
        compiler_params=_params(1),
        name="even_in",
    )(x, prep["w_in"], prep["gq"], prep["gkv"], prep["w_uq"], prep["w_kv"], prep["vg"], prep["vb"], tab,
      prep["wgate"], prep["bgate"])


def _mla_prompt_kernel(q_ref, k_ref, v_ref, o_ref, *, tq):
    qi = pl.program_id(1)
    lane = lax.broadcasted_iota(jnp.int32, (tq, LANES), 1)
    row = lax.broadcasted_iota(jnp.int32, (tq, tq), 0)
    col = lax.broadcasted_iota(jnp.int32, (tq, tq), 1)
    diag_mask = col <= row

    for hp in range(H_A // 2):
        vs = slice(hp * LANES, (hp + 1) * LANES)
        outs = []
        for j in range(2):
            hs = slice((2 * hp + j) * LANES, (2 * hp + j + 1) * LANES)
            qh = q_ref[:, hs]

            def step(kb, carry, masked, hs=hs, qh=qh, vs=vs):
                m, l, acc = carry
                ks = pl.ds(pl.multiple_of(kb * tq, tq), tq)
                s = _dot_nt(qh, k_ref[ks, hs])
                if masked:
                    s = jnp.where(diag_mask, s, NEG)
                m_new = jnp.maximum(m, jnp.max(s, -1, keepdims=True))
                a = jnp.exp(m - m_new)
                p = jnp.exp(s - m_new)
                l = a * l + jnp.sum(p, -1, keepdims=True)
                acc = a * acc + _dot(p.astype(BF16), v_ref[ks, vs])
                return m_new, l, acc

            init = (jnp.full((tq, 1), NEG, F32), jnp.zeros((tq, 1), F32), jnp.zeros((tq, LANES), F32))
            carry = lax.fori_loop(0, qi, functools.partial(step, masked=False), init)
            m, l, acc = step(qi, carry, True)
            outs.append(acc / l)
        o_ref[:, vs] = jnp.where(lane < DV_A, outs[0], outs[1]).astype(BF16)


def _mla_prompt(q, k, v, n_batch, seq, tq):
    nq = seq // tq
    return pl.pallas_call(
        functools.partial(_mla_prompt_kernel, tq=tq),
        grid=(n_batch, nq),
        in_specs=[pl.BlockSpec((tq, H_A * LANES), lambda b, i: (b * nq + i, 0)),
                  pl.BlockSpec((seq, H_A * LANES), lambda b, i: (b, 0)),
                  pl.BlockSpec((seq, H_A * DV_A), lambda b, i: (b, 0))],
        out_specs=pl.BlockSpec((tq, H_A * DV_A), lambda b, i: (b * nq + i, 0)),
        out_shape=jax.ShapeDtypeStruct((n_batch * seq, H_A * DV_A), BF16),
        compiler_params=_params(2),
        name="mla_prompt",
    )(q, k, v)


Q_LAT_W = D_C + LANES


def _mla_sample_kernel(pt_ref, q_ref, ckvn_ref, kpen_ref, wq_ref, wuv_ref, *rest, pp, t_dec):
    ckv_refs = rest[:pp]
    kpe_refs = rest[pp:2 * pp]
    o_ref = rest[2 * pp]
    qlat_ref, qrope_ref, m_ref, l_ref, acc_ref = rest[2 * pp + 1:]
    c = pl.program_id(1)
    rows = H_A * t_dec

    @pl.when(c == 0)
    def _():
        q = q_ref[...]
        qs = jnp.concatenate([q[:, h * LANES:(h + 1) * LANES] for h in range(H_A)], axis=0).astype(BF16)
        full = _dot(qs, wq_ref[...])
        ql = jnp.concatenate(
            [full[h * t_dec:(h + 1) * t_dec, h * Q_LAT_W:(h + 1) * Q_LAT_W] for h in range(H_A)], axis=0)
        qlat_ref[...] = ql[:, :D_C].astype(BF16)
        qrope_ref[...] = ql[:, D_C:].astype(BF16)
        m_ref[...] = jnp.full(m_ref.shape, NEG, F32)
        l_ref[...] = jnp.zeros(l_ref.shape, F32)
        acc_ref[...] = jnp.zeros(acc_ref.shape, F32)

    qlat = qlat_ref[...]
    qrope = qrope_ref[...][:, :DR_A]

    def update(s, vals):
        m_old = m_ref[...]
        m_new = jnp.maximum(m_old, jnp.max(s, -1, keepdims=True))
        a = jnp.exp(m_old - m_new)
        p = jnp.exp(s - m_new)
        l_ref[...] = a * l_ref[...] + jnp.sum(p, -1, keepdims=True)
        pb = p.astype(BF16)
        acc = a * acc_ref[...]
        for i, val in enumerate(vals):
            acc = acc + _dot(pb[:, i * PAGE_SIZE:(i + 1) * PAGE_SIZE], val)
        acc_ref[...] = acc
        m_ref[...] = m_new

    cks = [r[...].astype(BF16) for r in ckv_refs]
    s = jnp.concatenate(
        [_dot_nt(qlat, ck) + _dot_nt(qrope, kr[...].astype(BF16)) for ck, kr in zip(cks, kpe_refs)], axis=1)
    update(s, cks)

    @pl.when(c == pl.num_programs(1) - 1)
    def _():
        pad = PAGE_SIZE - t_dec
        cn = jnp.concatenate([ckvn_ref[...], jnp.zeros((pad, D_C), F32)], axis=0).astype(BF16)
        kn = jnp.concatenate([kpen_ref[...][:, :DR_A], jnp.zeros((pad, DR_A), F32)], axis=0).astype(BF16)
        s_new = _dot_nt(qlat, cn) + _dot_nt(qrope, kn)
        r = lax.broadcasted_iota(jnp.int32, (rows, PAGE_SIZE), 0) % t_dec
        cc = lax.broadcasted_iota(jnp.int32, (rows, PAGE_SIZE), 1)
        update(jnp.where(cc <= r, s_new, NEG), [cn])
        o_lat = (acc_ref[...] / l_ref[...]).astype(BF16)
        full = _dot(o_lat, wuv_ref[...])
        rh = lax.broadcasted_iota(jnp.int32, full.shape, 0) // t_dec
        ch = lax.broadcasted_iota(jnp.int32, full.shape, 1) // DV_A
        full = jnp.where(rh == ch, full, 0.0)
        out = full[0:t_dec]
        for h in range(1, H_A):
            out = out + full[h * t_dec:(h + 1) * t_dec]
        o_ref[...] = out


def _mla_sample(page_table, q_s, ckv_new, kpe_new, wq2, wuv, cache_ckv, cache_kpe, pp):
    nb, t_dec = q_s.shape[:2]
    n_pages = page_table.shape[1]
    assert n_pages % pp == 0
    rows = H_A * t_dec

    def page_map(p):
        return lambda b, c, pt: (0, pt[b * n_pages + c * pp + p], 0, 0)

    per_b = lambda w: pl.BlockSpec((None, t_dec, w), lambda b, c, pt: (b, 0, 0))
    grid_spec = pltpu.PrefetchScalarGridSpec(
        num_scalar_prefetch=1,
        grid=(nb, n_pages // pp),
        in_specs=[per_b(H_A * LANES), per_b(D_C), per_b(LANES),
                  pl.BlockSpec(wq2.shape, lambda b, c, pt: (0, 0)),
                  pl.BlockSpec(wuv.shape, lambda b, c, pt: (0, 0))]
                 + [pl.BlockSpec((None, None, PAGE_SIZE, D_C), page_map(p)) for p in range(pp)]
                 + [pl.BlockSpec((None, None, PAGE_SIZE, DR_A), page_map(p)) for p in range(pp)],
        out_specs=per_b(H_A * DV_A),
        scratch_shapes=[pltpu.VMEM((rows, D_C), BF16), pltpu.VMEM((rows, LANES), BF16),
                        pltpu.VMEM((rows, 1), F32), pltpu.VMEM((rows, 1), F32), pltpu.VMEM((rows, D_C), F32)],
    )
    return pl.pallas_call(
        functools.partial(_mla_sample_kernel, pp=pp, t_dec=t_dec),
        grid_spec=grid_spec,
        out_shape=jax.ShapeDtypeStruct((nb, t_dec, H_A * DV_A), F32),
        compiler_params=_params(2),
        name="mla_sample",
    )(page_table.reshape(-1), q_s, ckv_new, kpe_new, wq2, wuv, *([cache_ckv] * pp), *([cache_kpe] * pp))


def _proj_ln_kernel(x_ref, a_ref, b_ref, wa_ref, wb_ref, g_ref, beta_ref, o_ref):
    y = _dot(a_ref[...].astype(BF16), wa_ref[...]) + _dot(b_ref[...], wb_ref[...])
    o_ref[...] = _layer_norm(ALPHA * x_ref[...] + y, g_ref[...], beta_ref[...])


def _proj_ln(x, a, b, wa, wb, g, beta, tm):
    m, d = x.shape
    row = lambda arr: pl.BlockSpec((tm, arr.shape[1]), lambda i: (i, 0))
    full = lambda arr: pl.BlockSpec(arr.shape, lambda i: (0, 0))
    return pl.pallas_call(
        _proj_ln_kernel,
        grid=(m // tm,),
        in_specs=[row(x), row(a), row(b), full(wa), full(wb), full(g), full(beta)],
        out_specs=pl.BlockSpec((tm, d), lambda i: (i, 0)),
        out_shape=jax.ShapeDtypeStruct((m, d), F32),
        compiler_params=_params(1),
        name="mix_out_ln",
    )(x, a, b, wa, wb, g, beta)


def _ffn_kernel(x_ref, wg_ref, wu_ref, wd_ref, g_ref, beta_ref, o_ref, *, n_chunks):
    x = x_ref[...]
    xb = x.astype(BF16)
    fc = wg_ref.shape[1] // n_chunks
    y = jnp.zeros(x.shape, F32)
    for c in range(n_chunks):
        cs = slice(c * fc, (c + 1) * fc)
        h = jax.nn.silu(_dot(xb, wg_ref[:, cs])) * _dot(xb, wu_ref[:, cs])
        y = y + _dot(h.astype(BF16), wd_ref[cs, :])
    o_ref[...] = _layer_norm(ALPHA * x + y, g_ref[...], beta_ref[...])


def _ffn(x, wg, wu, wd, g, beta, tm, n_chunks):
    m, d = x.shape
    resident = lambda arr: pl.BlockSpec(arr.shape, lambda i: (0, 0), pipeline_mode=pl.Buffered(1))
    return pl.pallas_call(
        functools.partial(_ffn_kernel, n_chunks=n_chunks),
        grid=(m // tm,),
        in_specs=[pl.BlockSpec((tm, d), lambda i: (i, 0)), resident(wg), resident(wu), resident(wd),
                  resident(g), resident(beta)],
        out_specs=pl.BlockSpec((tm, d), lambda i: (i, 0)),
        out_shape=jax.ShapeDtypeStruct((m, d), F32),
        compiler_params=_params(1),
        name="ffn_ln",
    )(x, wg, wu, wd, g, beta)


def _mm_kernel(x_ref, w_ref, *o_refs, scale):
    y = _dot(x_ref[...].astype(BF16), w_ref[...])
    o_refs[0][...] = (y * scale).astype(BF16)
    if len(o_refs) > 1:
        o_refs[1][...] = y


def _mm(x, w, tm, tn, scale, with_f32):
    m, kdim = x.shape
    n = w.shape[1]
    outs = [jax.ShapeDtypeStruct((m, n), BF16)] + ([jax.ShapeDtypeStruct((m, n), F32)] if with_f32 else [])
    return pl.pallas_call(
        functools.partial(_mm_kernel, scale=scale),
        grid=(m // tm, n // tn),
        in_specs=[pl.BlockSpec((tm, kdim), lambda i, j: (i, 0)), pl.BlockSpec((kdim, tn), lambda i, j: (0, j))],
        out_specs=[pl.BlockSpec((tm, tn), lambda i, j: (i, j)) for _ in outs],
        out_shape=outs,
        compiler_params=_params(2),
        name="odd_in_f32" if with_f32 else "odd_in_q",
    )(x, w)


def _dsw_prompt_kernel(q_ref, k_ref, v_ref, bias_ref, o_ref, lse_ref, *, n_blk):
    blk = C_BLOCK
    lane = lax.broadcasted_iota(jnp.int32, (blk, LANES), 1)
    lo = lane < DH_C

    def block(n, first):
        qs = pl.ds(pl.multiple_of(n * blk, blk), blk)
        if first:
            ks = pl.ds(0, blk)
        else:
            ks = pl.ds(pl.multiple_of((n - 1) * blk, blk), 2 * blk)
        lse_tile = jnp.zeros((blk, LANES), F32)
        for hp in range(H_C // 2):
            hs = slice(hp * LANES, (hp + 1) * LANES)
            qp = q_ref[qs, hs]
            kp = k_ref[ks, hs]
            vp = v_ref[ks, hs]
            outs = []
            for j in range(2):
                h = 2 * hp + j
                qm = jnp.where(lo if j == 0 else ~lo, qp, jnp.zeros_like(qp))
                bias = bias_ref[h][:, blk:] if first else bias_ref[h]
                s = _dot_nt(qm, kp) + bias
                m = jnp.max(s, -1, keepdims=True)
                p = jnp.exp(s - m)
                l = jnp.sum(p, -1, keepdims=True)
                outs.append(_dot((p / l).astype(BF16), vp))
                lse_tile = jnp.where(lane == h, m + jnp.log(l), lse_tile)
            o_ref[qs, hs] = jnp.where(lo, outs[0], outs[1]).astype(BF16)
        lse_ref[qs, :] = lse_tile

    block(0, True)
    if n_blk > 1:
        def body(n, carry):
            block(n, False)
            return carry
        lax.fori_loop(1, n_blk, body, 0)


def _dsw_prompt(qb, kvb, bias, g, n_batch, seq):
    d = C_DILATIONS[g]
    length = seq // d
    assert length % C_BLOCK == 0
    m = qb.shape[0]
    w = H_C * DH_C
    ng = len(C_DILATIONS)
    q2 = qb.reshape(m // d, d * ng * w)
    kv2 = kvb.reshape(m // d, d * 2 * ng * w)
    outs = [jax.ShapeDtypeStruct((m // d, d * w), BF16), jax.ShapeDtypeStruct((m // d, d * LANES), F32)]
    o, lse = pl.pallas_call(
        functools.partial(_dsw_prompt_kernel, n_blk=length // C_BLOCK),
        grid=(n_batch, d),
        in_specs=[pl.BlockSpec((length, w), lambda b, r: (b, r * ng + g)),
                  pl.BlockSpec((length, w), lambda b, r: (b, r * 2 * ng + 2 * g)),
                  pl.BlockSpec((length, w), lambda b, r: (b, r * 2 * ng + 2 * g + 1)),
                  pl.BlockSpec(bias.shape, lambda b, r: (0, 0, 0))],
        out_specs=[pl.BlockSpec((length, w), lambda b, r: (b, r)),
                   pl.BlockSpec((length, LANES), lambda b, r: (b, r))],
        out_shape=outs,
        compiler_params=_params(2),
        name=f"dsw_prompt_g{g}",
    )(q2, kv2, kv2, bias)
    return o.reshape(m, w), lse.reshape(m, LANES)


def _dsw_sample_kernel(q_ref, kvn_ref, buf_ref, mask_ref, maskn_ref, o_in_ref, lse_in_ref, o_ref, lse_ref, *,
                       t_dec, n_seq):
    del o_in_ref, lse_in_ref
    w = H_C * DH_C
    rows = H_C * t_dec
    rh = lax.broadcasted_iota(jnp.int32, (rows, w), 0) // t_dec
    ch = lax.broadcasted_iota(jnp.int32, (rows, w), 1) // DH_C
    same = rh == ch
    lane = lax.broadcasted_iota(jnp.int32, (rows, LANES), 1)
    rl = lax.broadcasted_iota(jnp.int32, (rows, LANES), 0) // t_dec
    pad = LANES - t_dec
    for i in range(n_seq):
        q = q_ref[i]
        qbd = jnp.where(same, jnp.concatenate([q] * H_C, axis=0), 0.0).astype(BF16)
        buf = buf_ref[i].reshape(-1, 2 * w)
        kb = buf[:, :w].astype(BF16)
        vb = buf[:, w:].astype(BF16)
        kvn = kvn_ref[i]
        kn = jnp.concatenate([kvn[:, :w], jnp.zeros((pad, w), F32)], axis=0).astype(BF16)
        vn = jnp.concatenate([kvn[:, w:], jnp.zeros((pad, w), F32)], axis=0).astype(BF16)

        s_b = _dot_nt(qbd, kb) + mask_ref[...]
        s_n = _dot_nt(qbd, kn) + maskn_ref[...]
        m = jnp.maximum(jnp.max(s_b, -1, keepdims=True), jnp.max(s_n, -1, keepdims=True))
        p_b = jnp.exp(s_b - m)
        p_n = jnp.exp(s_n - m)
        l = jnp.sum(p_b, -1, keepdims=True) + jnp.sum(p_n, -1, keepdims=True)
        o = _dot((p_b / l).astype(BF16), vb) + _dot((p_n / l).astype(BF16), vn)
        o = jnp.where(same, o, 0.0)
        lse = jnp.where(lane == rl, m + jnp.log(l), 0.0)
        o_acc = o[0:t_dec]
        lse_acc = lse[0:t_dec]
        for h in range(1, H_C):
            o_acc = o_acc + o[h * t_dec:(h + 1) * t_dec]
            lse_acc = lse_acc + lse[h * t_dec:(h + 1) * t_dec]
        o_ref[i * t_dec:(i + 1) * t_dec, :] = o_acc.astype(o_ref.dtype)
        lse_ref[i * t_dec:(i + 1) * t_dec, :] = lse_acc


def _dsw_sample(q_s, kv_new, buf, mask, mask_new, o_all, lse_all, n_prompt_rows, sub, n_seq):
    nb, t_dec, w = q_s.shape
    assert nb % n_seq == 0
    if sub is None:
        buf_spec = pl.BlockSpec((n_seq,) + buf.shape[1:], lambda b: (b, 0, 0))
    else:
        buf_spec = pl.BlockSpec((n_seq, buf.shape[1], sub, buf.shape[3]), lambda b: (b, 0, 0, 0))
    rows = n_seq * t_dec
    base = n_prompt_rows // rows
    o, lse = pl.pallas_call(
        functools.partial(_dsw_sample_kernel, t_dec=t_dec, n_seq=n_seq),
        grid=(nb // n_seq,),
        in_specs=[pl.BlockSpec((n_seq, t_dec, w), lambda b: (b, 0, 0)),
                  pl.BlockSpec((n_seq, t_dec, 2 * w), lambda b: (b, 0, 0)),
                  buf_spec,
                  pl.BlockSpec(mask.shape, lambda b: (0, 0)),
                  pl.BlockSpec(mask_new.shape, lambda b: (0, 0)),
                  pl.BlockSpec(memory_space=pl.ANY), pl.BlockSpec(memory_space=pl.ANY)],
        out_specs=[pl.BlockSpec((rows, w), lambda b: (base + b, 0)),
                   pl.BlockSpec((rows, LANES), lambda b: (base + b, 0))],
        out_shape=[jax.ShapeDtypeStruct(o_all.shape, o_all.dtype),
                   jax.ShapeDtypeStruct(lse_all.shape, lse_all.dtype)],
        input_output_aliases={5: 0, 6: 1},
        compiler_params=_params(1),
        name="dsw_sample",
    )(q_s, kv_new, buf, mask, mask_new, o_all, lse_all)
    return o, lse


def _split_bf16(x):
    hi = x.astype(BF16)
    return hi, (x - hi.astype(F32)).astype(BF16)


def _merge_kernel(x_ref, o0_ref, o1_ref, o2_ref, l0_ref, l1_ref, l2_ref, e_ref, w_ref, g_ref, beta_ref,
                  wr_hi_ref, wr_lo_ref, y_ref, yb_ref, route_ref):
    lses = [l0_ref[...], l1_ref[...], l2_ref[...]]
    mx = jnp.maximum(jnp.maximum(lses[0], lses[1]), lses[2])
    es = [jnp.exp(l - mx) for l in lses]
    den = es[0] + es[1] + es[2]
    acc = None
    for e, o_ref in zip(es, (o0_ref, o1_ref, o2_ref)):
        hi, lo = _split_bf16(e / den)
        wexp = _dot(hi, e_ref[...]) + _dot(lo, e_ref[...])
        term = wexp * o_ref[...].astype(F32)
        acc = term if acc is None else acc + term
    y = _dot(acc.astype(BF16), w_ref[...])
    x2 = _layer_norm(ALPHA * x_ref[...] + y, g_ref[...], beta_ref[...])
    y_ref[...] = x2
    yb_ref[...] = x2.astype(BF16)

    xh, xl = _split_bf16(x2)
    logits = _dot(xh, wr_hi_ref[...]) + _dot(xl, wr_hi_ref[...]) + _dot(xh, wr_lo_ref[...])
    lane = lax.broadcasted_iota(jnp.int32, logits.shape, 1)
    logits = jnp.where(lane < N_EXPERTS, logits, -jnp.inf)
    v1 = jnp.max(logits, -1, keepdims=True)
    i1 = jnp.min(jnp.where(logits == v1, lane, LANES), -1, keepdims=True)
    rest = jnp.where(lane == i1, -jnp.inf, logits)
    v2 = jnp.max(rest, -1, keepdims=True)
    i2 = jnp.min(jnp.where(rest == v2, lane, LANES), -1, keepdims=True)
    e2 = jnp.exp(v2 - v1)
    g1 = 1.0 / (1.0 + e2)
    g2 = e2 / (1.0 + e2)
    route = jnp.where(lane == 0, i1.astype(F32), 0.0)
    route = jnp.where(lane == 1, i2.astype(F32), route)
    route = jnp.where(lane == 2, g1, route)
    route = jnp.where(lane == 3, g2, route)
    route_ref[...] = route


def _merge(x, os_, lses, expand, w_out, g, beta, wr_hi, wr_lo, tm):
    m, d = x.shape
    row = lambda arr: pl.BlockSpec((tm, arr.shape[1]), lambda i: (i, 0))
    full = lambda arr: pl.BlockSpec(arr.shape, lambda i: (0, 0))
    outs = [jax.ShapeDtypeStruct((m, d), F32), jax.ShapeDtypeStruct((m, d), BF16),
            jax.ShapeDtypeStruct((m, LANES), F32)]
    return pl.pallas_call(
        _merge_kernel,
        grid=(m // tm,),
        in_specs=[row(x)] + [row(o) for o in os_] + [row(l) for l in lses]
                 + [full(expand), full(w_out), full(g), full(beta), full(wr_hi), full(wr_lo)],
        out_specs=[row(o) for o in outs],
        out_shape=outs,
        compiler_params=_params(1),
        name="merge_out_ln_route",
    )(x, *os_, *lses, expand, w_out, g, beta, wr_hi, wr_lo)


def _moe_kernel(te_ref, nt_ref, x_ref, wg_ref, wu_ref, wd_ref, o_ref, *, n_chunks):
    t = pl.program_id(0)

    @pl.when(t < nt_ref[0])
    def _():
        xb = x_ref[...]
        fc = wg_ref.shape[1] // n_chunks
        y = jnp.zeros(o_ref.shape, F32)
        for c in range(n_chunks):
            cs = slice(c * fc, (c + 1) * fc)
            h = jax.nn.silu(_dot(xb, wg_ref[:, cs])) * _dot(xb, wu_ref[:, cs])
            y = y + _dot(h.astype(BF16), wd_ref[cs, :])
        o_ref[...] = y

    @pl.when(t >= nt_ref[0])
    def _():
        o_ref[...] = jnp.zeros(o_ref.shape, F32)


def _moe(tile_expert, n_tiles_used, xs, wg, wu, wd, tm, n_chunks):
    p, d = xs.shape
    f = wg.shape[2]
    grid_spec = pltpu.PrefetchScalarGridSpec(
        num_scalar_prefetch=2,
        grid=(p // tm,),
        in_specs=[pl.BlockSpec((tm, d), lambda t, te, nt: (t, 0)),
                  pl.BlockSpec((None, d, f), lambda t, te, nt: (te[t], 0, 0)),
                  pl.BlockSpec((None, d, f), lambda t, te, nt: (te[t], 0, 0)),
                  pl.BlockSpec((None, f, d), lambda t, te, nt: (te[t], 0, 0))],
        out_specs=pl.BlockSpec((tm, d), lambda t, te, nt: (t, 0)),
    )
    return pl.pallas_call(
        functools.partial(_moe_kernel, n_chunks=n_chunks),
        grid_spec=grid_spec,
        out_shape=jax.ShapeDtypeStruct((p, d), F32),
        compiler_params=_params(1),
        name="moe_experts",
    )(tile_expert, n_tiles_used, xs, wg, wu, wd)


def _final_kernel(x_ref, ya_ref, yb_ref, route_ref, g_ref, beta_ref, o_ref):
    route = route_ref[...]
    y = route[:, 2:3] * ya_ref[...] + route[:, 3:4] * yb_ref[...]
    o_ref[...] = _layer_norm(ALPHA * x_ref[...] + y, g_ref[...], beta_ref[...])


def _final(x, ya, yb, route, g, beta, tm):
    m, d = x.shape
    row = lambda arr: pl.BlockSpec((tm, arr.shape[1]), lambda i: (i, 0))
    full = lambda arr: pl.BlockSpec(arr.shape, lambda i: (0, 0))
    return pl.pallas_call(
        _final_kernel,
        grid=(m // tm,),
        in_specs=[row(x), row(ya), row(yb), row(route), full(g), full(beta)],
        out_specs=row(x),
        out_shape=jax.ShapeDtypeStruct((m, d), F32),
        compiler_params=_params(1),
        name="moe_combine_ln",
    )(x, ya, yb, route, g, beta)


def _rope_table(pos, scale):
    half = DR_A // 2
    inv = ROPE_BASE ** (-jnp.arange(half, dtype=F32) / half)
    ang = pos.astype(F32)[:, None] * inv[None, :]
    cos, sin = jnp.cos(ang), jnp.sin(ang)
    n = pos.shape[0]
    z = lambda w: jnp.zeros((n, w), F32)
    one = jnp.ones((n, DN_A), F32)
    cq = jnp.concatenate([one, cos, cos, z(32)], 1) * scale
    s1q = jnp.concatenate([z(DN_A), -sin, z(16), z(32)], 1) * scale
    s2q = jnp.concatenate([z(DN_A), z(16), sin, z(32)], 1) * scale
    ck = jnp.concatenate([cos, cos, z(96)], 1)
    s1k = jnp.concatenate([-sin, z(112)], 1)
    s2k = jnp.concatenate([z(16), sin, z(96)], 1)
    return jnp.concatenate([cq, s1q, s2q, ck, s1k, s2k], 1)


def _t5_bucket(dist):
    max_exact = N_BUCKETS // 2
    log_ratio = jnp.log(jnp.maximum(dist, 1).astype(F32) / max_exact) / math.log(MAX_DISTANCE / max_exact)
    large = jnp.minimum(max_exact + (log_ratio * (N_BUCKETS - max_exact)).astype(jnp.int32), N_BUCKETS - 1)
    return jnp.where(dist < max_exact, dist, large)


def _group_bias(rel_bias, g):
    d = C_DILATIONS[g]
    dist = d * jnp.arange(C_WINDOWS[g] // d + 1, dtype=jnp.int32)
    return jnp.transpose(rel_bias[_t5_bucket(dist), g * H_C:(g + 1) * H_C])


def _prompt_bias(bias):
    n_back = bias.shape[1] - 1
    a = np.arange(C_BLOCK)[:, None]
    c = np.arange(2 * C_BLOCK)[None, :]
    j = a + C_BLOCK - c
    band = (j >= 0) & (j <= n_back)
    return jnp.where(band[None], bias[:, np.clip(j, 0, n_back)].astype(F32), NEG)


def _sample_masks(bias, g, n_buf, t_dec, rows_used):
    d = C_DILATIONS[g]
    n_back = bias.shape[1] - 1
    t = np.arange(t_dec)[:, None]

    def build(key_pos):
        diff = n_buf + t - key_pos[None, :]
        ok = (diff >= 0) & (diff % d == 0) & (diff // d <= n_back)
        j = np.clip(diff // d, 0, n_back)
        mt = jnp.where(ok[None], bias[:, j].astype(F32), NEG)
        return mt.reshape(H_C * t_dec, -1)

    new_pos = n_buf + np.arange(LANES)
    mask_new = build(new_pos)
    col_ok = (np.arange(LANES) < t_dec)[None, :]
    mask_new = jnp.where(col_ok, mask_new, NEG)
    return build(rows_used), mask_new


def _prep_even(e_w_in, e_q_norm_g, e_kv_norm_g, e_w_uq, e_w_uk, e_w_uv, e_v_ln_g, e_v_ln_b, e_w_s, e_b_s, t_dec):
    o_kr = D_CQ + D_C
    o_uv = o_kr + DR_A
    d_model = e_w_in.shape[0]
    w_in = jnp.concatenate([e_w_in[:, :o_kr], e_w_in[:, o_uv:], e_w_in[:, o_kr:o_uv],
                            jnp.zeros((d_model, LANES - DR_A), F32)], axis=1).astype(BF16)
    w_uq = jnp.pad(e_w_uq, ((0, 0), (0, 0), (0, LANES - DN_A - DR_A))).reshape(D_CQ, H_A * LANES).astype(BF16)
    w_uk_pad = jnp.pad(e_w_uk, ((0, 0), (0, 0), (0, LANES - DN_A))).reshape(D_C, H_A * LANES)
    w_uv = e_w_uv.reshape(D_C, H_A * DV_A)
    place = np.zeros((LANES, H_A, LANES), np.float32)
    for i in range(DR_A):
        place[i, :, DN_A + i] = 1.0
    place = place.reshape(LANES, H_A * LANES)
    w_kv = jnp.concatenate([
        jnp.concatenate([w_uk_pad, w_uv], axis=1),
        jnp.concatenate([jnp.asarray(place), jnp.zeros((LANES, H_A * DV_A), F32)], axis=1)], axis=0).astype(BF16)

    tril = np.tril(np.ones((CHUNK, CHUNK), bool))
    wg_p = jnp.where(tril[None], e_w_s, 0.0)
    small = jnp.where(np.tril(np.ones((t_dec, t_dec), bool))[None], e_w_s[:, :t_dec, :t_dec], 0.0)
    eye = jnp.eye(CHUNK // t_dec, dtype=F32)
    wg_s = jnp.einsum("ab,gts->gatbs", eye, small).reshape(G_B, CHUNK, CHUNK)
    wgate = jnp.stack([wg_p, wg_s]).astype(BF16)
    bg_p = jnp.broadcast_to(e_b_s[:, :, None], (G_B, CHUNK, LANES))
    bg_s = jnp.broadcast_to(jnp.tile(e_b_s[:, :t_dec], (1, CHUNK // t_dec))[:, :, None], (G_B, CHUNK, LANES))
    bgate = jnp.stack([bg_p, bg_s]).astype(F32)

    wq2 = np.zeros((LANES, H_A, Q_LAT_W), np.float32)
    sel = np.zeros((LANES, Q_LAT_W), np.float32)
    for i in range(DR_A):
        sel[DN_A + i, D_C + i] = 1.0
    wq2 = jnp.asarray(wq2) + jnp.asarray(sel)[:, None, :]
    uk_t = jnp.transpose(e_w_uk, (2, 1, 0))
    wq2 = wq2.at[:DN_A, :, :D_C].add(uk_t)
    wq2 = wq2.reshape(LANES, H_A * Q_LAT_W).astype(BF16)
    return dict(w_in=w_in, gq=e_q_norm_g.reshape(1, -1), gkv=e_kv_norm_g.reshape(1, -1), w_uq=w_uq, w_kv=w_kv,
                vg=e_v_ln_g.reshape(1, -1), vb=e_v_ln_b.reshape(1, -1), wgate=wgate, bgate=bgate,
                wq2=wq2, wuv=w_uv.astype(BF16))


def _pick_tile(m_prompt, m_sample, cap):
    t = cap
    while m_prompt % t or m_sample % t:
        t //= 2
    assert t >= CHUNK
    return t


def kernel(x_prompt, x_sample, cache_mla_ckv, cache_mla_kpe, cache_dsw_kv0, cache_dsw_kv1, cache_dsw_kv2, page_table, ln_g, ln_b, e_w_in, e_q_norm_g, e_kv_norm_g, e_w_uq, e_w_uk, e_w_uv, e_v_ln_g, e_v_ln_b, e_w_s, e_b_s, e_w_out, ffn_w_gate, ffn_w_up, ffn_w_down, o_w_in, o_w_out, rel_bias, moe_w_router, moe_w_gate, moe_w_up, moe_w_down):
    nb_p, seq, d_model = x_prompt.shape
    nb_s, t_dec, _ = x_sample.shape
    past_len = page_table.shape[1] * PAGE_SIZE
    mp, ms = nb_p * seq, nb_s * t_dec
    m = mp + ms
    tm = _pick_tile(seq, ms, 512)
    assert CHUNK % t_dec == 0 and seq % CHUNK == 0
    ffn_chunks = 4

    x0 = jnp.concatenate([x_prompt.reshape(mp, d_model), x_sample.reshape(ms, d_model)], axis=0)
    ln = lambda l, j: (ln_g[l, j].reshape(1, -1), ln_b[l, j].reshape(1, -1))

    prep = _prep_even(e_w_in[0], e_q_norm_g[0], e_kv_norm_g[0], e_w_uq[0], e_w_uk[0], e_w_uv[0], e_v_ln_g[0],
                      e_v_ln_b[0], e_w_s[0], e_b_s[0], t_dec)
    scale = (DN_A + DR_A) ** -0.5
    pos = jnp.concatenate([jnp.arange(seq, dtype=jnp.int32),
                           past_len + (jnp.arange(tm, dtype=jnp.int32) % t_dec)])
    tab = _rope_table(pos, scale)
    q, k, v, ckv, kpe, gated, vln = _even_in(x0, prep, tab, tm, mp // tm, seq // tm)

    att_p = _mla_prompt(q, k, v, nb_p, seq, min(256, seq))
    q_s = q[mp:].astype(F32).reshape(nb_s, t_dec, -1)
    ckv_s = ckv[mp:].reshape(nb_s, t_dec, D_C)
    kpe_s = kpe[mp:].reshape(nb_s, t_dec, LANES)
    pp = 8 if page_table.shape[1] % 8 == 0 else 1
    att_s = _mla_sample(page_table, q_s, ckv_s, kpe_s, prep["wq2"], prep["wuv"], cache_mla_ckv, cache_mla_kpe, pp)
    att = jnp.concatenate([att_p.astype(F32), att_s.reshape(ms, -1)], axis=0)

    w_out_e = e_w_out[0].astype(BF16)
    n_att = H_A * DV_A
    x1 = _proj_ln(x0, att, gated, w_out_e[:n_att], w_out_e[n_att:], *ln(0, 0), tm)
    x1 = _ffn(x1, ffn_w_gate[0].astype(BF16), ffn_w_up[0].astype(BF16), ffn_w_down[0].astype(BF16), *ln(0, 1),
              tm, ffn_chunks)

    ng = len(C_DILATIONS)
    w = H_C * DH_C
    w_in_o = o_w_in[0].reshape(d_model, ng, 3, w)
    w_q = w_in_o[:, :, 0].reshape(d_model, ng * w).astype(BF16)
    w_kv = w_in_o[:, :, 1:].reshape(d_model, ng * 2 * w).astype(BF16)
    (qb,) = _mm(x1, w_q, tm, w, DH_C ** -0.5, False)
    kvb, kvf = _mm(x1, w_kv, tm, w, 1.0, True)

    biases = [_group_bias(rel_bias, g) for g in range(ng)]
    caches = (cache_dsw_kv0, cache_dsw_kv1, cache_dsw_kv2)
    q_s = qb[mp:].astype(F32).reshape(nb_s, t_dec, ng, w)
    kvf_s = kvf[mp:].reshape(nb_s, t_dec, ng, 2 * w)
    os_, lses, dsw_s = [], [], []
    for g in range(ng):
        d = C_DILATIONS[g]
        o_g, lse_g = _dsw_prompt(qb, kvb, _prompt_bias(biases[g]), g, nb_p, seq)
        n_buf = caches[g].shape[2]
        buf = caches[g][0].reshape(nb_s, n_buf, 2 * w)
        if d > t_dec and n_buf % d == 0 and d % 8 == 0:
            sub = 8
            buf = buf.reshape(nb_s, n_buf // d, d, 2 * w)
            rows_used = (np.arange(n_buf // d)[:, None] * d + np.arange(sub)[None, :]).reshape(-1)
        else:
            sub = None
            rows_used = np.arange(n_buf)
        mask, mask_new = _sample_masks(biases[g], g, n_buf, t_dec, rows_used)
        kv_new = kvf_s[:, :, g]
        dsw_s.append(kv_new.reshape(1, nb_s, t_dec, 2, H_C, DH_C))
        o_g, lse_g = _dsw_sample(q_s[:, :, g], kv_new, buf, mask, mask_new, o_g, lse_g, mp, sub, 2)
        os_.append(o_g)
        lses.append(lse_g)

    expand = np.zeros((LANES, w), np.float32)
    for h in range(H_C):
        expand[h, h * DH_C:(h + 1) * DH_C] = 1.0
    wr = jnp.pad(moe_w_router[0], ((0, 0), (0, LANES - N_EXPERTS)))
    wr_hi, wr_lo = _split_bf16(wr)
    x2, x2b, route = _merge(x1, os_, lses, jnp.asarray(expand, BF16), o_w_out[0].astype(BF16), *ln(1, 0),
                            wr_hi, wr_lo, tm)

    tmoe = 256
    idx = route[:, :TOP_K].astype(jnp.int32)
    flat_e = idx.reshape(-1)
    onehot = (flat_e[:, None] == jnp.arange(N_EXPERTS)[None, :]).astype(jnp.int32)
    csum = jnp.cumsum(onehot, axis=0)
    counts = csum[-1]
    rank = jnp.take_along_axis(csum, flat_e[:, None], axis=1)[:, 0] - 1
    padded = ((counts + tmoe - 1) // tmoe) * tmoe
    ends = jnp.cumsum(padded)
    starts = ends - padded
    pos_sorted = starts[flat_e] + rank
    p_rows = ((TOP_K * m + tmoe - 1) // tmoe + N_EXPERTS) * tmoe
    n_tiles = p_rows // tmoe
    src = jnp.zeros((p_rows,), jnp.int32).at[pos_sorted].set(jnp.arange(TOP_K * m, dtype=jnp.int32) // TOP_K)
    xs = jnp.take(x2b, src, axis=0)
    tile_start = jnp.arange(n_tiles, dtype=jnp.int32) * tmoe
    tile_expert = jnp.minimum(jnp.sum(tile_start[:, None] >= ends[None, :], axis=1), N_EXPERTS - 1).astype(jnp.int32)
    n_used = (ends[-1] // tmoe).astype(jnp.int32).reshape(1)
    ys = _moe(tile_expert, n_used, xs, moe_w_gate[0].astype(BF16), moe_w_up[0].astype(BF16),
              moe_w_down[0].astype(BF16), tmoe, ffn_chunks)
    pos2 = pos_sorted.reshape(m, TOP_K)
    ya = jnp.take(ys, pos2[:, 0], axis=0)
    yb = jnp.take(ys, pos2[:, 1], axis=0)
    y = _final(x2, ya, yb, route, *ln(1, 1), tm)

    y_prompt = y[:mp].reshape(nb_p, seq, d_model)
    y_sample = y[mp:].reshape(nb_s, t_dec, d_model)
    n_pg = seq // PAGE_SIZE
    new_ckv_prompt = ckv[:mp].reshape(1, nb_p, n_pg, PAGE_SIZE, D_C)
    new_kpe_prompt = kpe[:mp, :DR_A].reshape(1, nb_p, n_pg, PAGE_SIZE, DR_A)
    new_ckv_sample = ckv_s.reshape(1, nb_s, t_dec, D_C)
    new_kpe_sample = kpe_s[:, :, :DR_A].reshape(1, nb_s, t_dec, DR_A)
    new_gate_v_sample = vln[mp:].reshape(1, nb_s, t_dec, D_B)
    kvf_p = kvf[:mp].reshape(nb_p, seq, ng, 2, H_C, DH_C)
    dsw_p = [kvf_p[:, seq - min(C_WINDOWS[g], seq):, g][None] for g in range(ng)]
    return (y_prompt, y_sample, new_ckv_prompt, new_kpe_prompt, new_ckv_sample, new_kpe_sample, new_gate_v_sample,
            dsw_p[0], dsw_p[1], dsw_p[2], dsw_s[0], dsw_s[1], dsw_s[2])
```

```python
import functools
import math

import numpy as np
import jax
import jax.numpy as jnp
from jax import lax
from jax.experimental import pallas as pl
from jax.experimental.pallas import tpu as pltpu

BF16 = jnp.bfloat16
F32 = jnp.float32

H_A, DN_A, DR_A, DV_A = 8, 64, 32, 64
D_CQ, D_C = 384, 256
ROPE_BASE = 10000.0
CHUNK, G_B, D_B = 128, 4, 512
C_WINDOWS = (128, 512, 2048)
C_DILATIONS = (1, 4, 16)
N_GROUPS_C = 3
H_C, DH_C = 16, 64
W_C = H_C * DH_C
C_BLOCK = 128
N_BUCKETS, MAX_DISTANCE = 32, 2048
N_EXPERTS, TOP_K = 8, 2
DEPTH = 2
ALPHA = (2 * DEPTH) ** 0.25
LN_EPS = 1e-5
RMS_EPS = 1e-6
PAGE_SIZE = 128
NEG = -1e30

LANES = 128
V7X_VMEM_BYTES = 64 * 2**20
VMEM_LIMIT = 56 * 2**20


def _dot(a, b):
    return jnp.dot(a, b, preferred_element_type=F32)


def _dot_nt(a, b):
    return lax.dot_general(a, b, (((1,), (1,)), ((), ())), preferred_element_type=F32)


def _params(n_axes, vmem=VMEM_LIMIT):
    return pltpu.CompilerParams(dimension_semantics=("arbitrary",) * n_axes, vmem_limit_bytes=vmem)


def _layer_norm(x, g, b):
    mu = jnp.mean(x, -1, keepdims=True)
    xc = x - mu
    var = jnp.mean(xc * xc, -1, keepdims=True)
    return xc * lax.rsqrt(var + LN_EPS) * g + b


def _rms_norm(x, g):
    return x * lax.rsqrt(jnp.mean(x * x, -1, keepdims=True) + RMS_EPS) * g


def _full_spec(a):
    return pl.BlockSpec(a.shape, lambda *_: (0,) * a.ndim)


def _rope_apply(x, c, s1, s2):
    return x * c + pltpu.roll(x, LANES - 16, 1) * s1 + pltpu.roll(x, 16, 1) * s2


def _even_in_kernel(x_ref, w_in_ref, gq_ref, gkv_ref, w_uq_ref, w_kv_ref, vg_ref, vb_ref, tab_ref,
                    wgate_ref, bgate_ref,
                    q_ref, k_ref, v_ref, ckv_ref, kpe_ref, gated_ref, vln_ref):
    tm = x_ref.shape[0]
    h = _dot(x_ref[...].astype(BF16), w_in_ref[...])
    c_q = h[:, :D_CQ]
    c_kv = h[:, D_CQ:D_CQ + D_C]
    o_u = D_CQ + D_C
    u = jax.nn.gelu(h[:, o_u:o_u + D_B])
    vv = jax.nn.gelu(h[:, o_u + D_B:o_u + 2 * D_B])
    kr = h[:, o_u + 2 * D_B:]

    tab = tab_ref[...]
    cq, s1q, s2q = tab[:, 0:128], tab[:, 128:256], tab[:, 256:384]
    ck, s1k, s2k = tab[:, 384:512], tab[:, 512:640], tab[:, 640:768]

    q = _dot(_rms_norm(c_q, gq_ref[...]).astype(BF16), w_uq_ref[...])
    for hb in range(H_A):
        q_ref[hb] = _rope_apply(q[:, hb * LANES:(hb + 1) * LANES], cq, s1q, s2q).astype(BF16)

    ckv = _rms_norm(c_kv, gkv_ref[...])
    ckv_ref[...] = ckv
    kpe = _rope_apply(kr, ck, s1k, s2k)
    kpe_ref[...] = kpe
    kv = _dot(jnp.concatenate([ckv.astype(BF16), kpe.astype(BF16)], axis=1), w_kv_ref[...])
    for hb in range(H_A):
        k_ref[hb] = kv[:, hb * LANES:(hb + 1) * LANES].astype(BF16)
    for p in range(H_A // 2):
        v_ref[p] = kv[:, (H_A + p) * LANES:(H_A + p + 1) * LANES].astype(BF16)

    vln = _layer_norm(vv, vg_ref[...], vb_ref[...])
    vln_ref[...] = vln
    vb16 = vln.astype(BF16)
    gw = D_B // G_B
    for c in range(tm // CHUNK):
        rs = slice(c * CHUNK, (c + 1) * CHUNK)
        for g in range(G_B):
            cs = slice(g * gw, (g + 1) * gw)
            f = _dot(wgate_ref[g], vb16[rs, cs]) + bgate_ref[g]
            gated_ref[rs, cs] = (u[rs, cs] * f).astype(BF16)


def _even_in(x, prep, tab, tm, n_prompt_tiles, tab_blocks):
    m, d_model = x.shape

    def tab_map(i):
        return (jnp.where(i < n_prompt_tiles, i % tab_blocks, tab_blocks), 0)

    def gate_map(i):
        return (jnp.where(i < n_prompt_tiles, 0, 1), 0, 0, 0)

    row = lambda w: pl.BlockSpec((tm, w), lambda i: (i, 0))
    heads = lambda n: pl.BlockSpec((n, tm, LANES), lambda i: (0, i, 0))
    outs = [
        jax.ShapeDtypeStruct((H_A, m, LANES), BF16),
        jax.ShapeDtypeStruct((H_A, m, LANES), BF16),
        jax.ShapeDtypeStruct((H_A // 2, m, LANES), BF16),
        jax.ShapeDtypeStruct((m, D_C), F32),
        jax.ShapeDtypeStruct((m, LANES), F32),
        jax.ShapeDtypeStruct((m, D_B), BF16),
        jax.ShapeDtypeStruct((m, D_B), F32),
    ]
    weights = [prep[n] for n in ("w_in", "gq", "gkv", "w_uq", "w_kv", "vg", "vb")]
    return pl.pallas_call(
        _even_in_kernel,
        grid=(m // tm,),
        in_specs=[row(d_model)] + [_full_spec(a) for a in weights]
                 + [pl.BlockSpec((tm, 6 * LANES), tab_map),
                    pl.BlockSpec((None, G_B, CHUNK, CHUNK), gate_map),
                    pl.BlockSpec((None, G_B, CHUNK, LANES), gate_map)],
        out_specs=[heads(H_A), heads(H_A), heads(H_A // 2), row(D_C), row(LANES), row(D_B), row(D_B)],
        out_shape=outs,
        compiler_params=_params(1),
        name="even_in",
    )(x, *weights, tab, prep["wgate"], prep["bgate"])


def _mla_prompt_kernel(q_ref, k_ref, v_ref, o_ref, *, tq):
    seq = q_ref.shape[0]
    h = pl.program_id(1)
    lane = lax.broadcasted_iota(jnp.int32, (tq, LANES), 1)
    row = lax.broadcasted_iota(jnp.int32, (tq, tq), 0)
    col = lax.broadcasted_iota(jnp.int32, (tq, tq), 1)
    diag_mask = col <= row
    for i in range(seq // tq):
        cur = slice(i * tq, (i + 1) * tq)
        past = slice(0, i * tq)
        qi = q_ref[cur, :]
        s_d = jnp.where(diag_mask, _dot_nt(qi, k_ref[cur, :]), NEG)
        m = jnp.max(s_d, -1, keepdims=True)
        if i:
            s_p = _dot_nt(qi, k_ref[past, :])
            m = jnp.maximum(m, jnp.max(s_p, -1, keepdims=True))
        p_d = jnp.exp(s_d - m)
        l = jnp.sum(p_d, -1, keepdims=True)
        acc = _dot(p_d.astype(BF16), v_ref[cur, :])
        if i:
            p_p = jnp.exp(s_p - m)
            l = l + jnp.sum(p_p, -1, keepdims=True)
            acc = acc + _dot(p_p.astype(BF16), v_ref[past, :])
        o = (acc / l).astype(BF16)

        @pl.when(h % 2 == 0)
        def _():
            o_ref[cur, :] = o

        @pl.when(h % 2 == 1)
        def _():
            o_ref[cur, :] = jnp.where(lane < DV_A, o_ref[cur, :], o)


def _mla_prompt(q, k, v, n_batch, seq, tq):
    m = q.shape[1]
    per_head = lambda div: pl.BlockSpec((None, seq, LANES), lambda b, h: (h // div, b, 0))
    return pl.pallas_call(
        functools.partial(_mla_prompt_kernel, tq=tq),
        grid=(n_batch, H_A),
        in_specs=[per_head(1), per_head(1), per_head(2)],
        out_specs=per_head(2),
        out_shape=jax.ShapeDtypeStruct((H_A // 2, m, LANES), BF16),
        compiler_params=_params(2),
        name="mla_prompt",
    )(q, k, v)


Q_LAT_W = D_C + LANES


def _mla_sample_kernel(pt_ref, q_ref, ckvn_ref, kpen_ref, wq_ref, wuv_ref, *rest, n_pages, t_dec):
    del pt_ref
    ckv_refs = rest[:n_pages]
    kpe_refs = rest[n_pages:2 * n_pages]
    o_ref = rest[2 * n_pages]
    rows = H_A * t_dec

    q = q_ref[...]
    qs = jnp.concatenate([q[:, h * LANES:(h + 1) * LANES] for h in range(H_A)], axis=0).astype(BF16)
    full = _dot(qs, wq_ref[...])
    ql = jnp.concatenate(
        [full[h * t_dec:(h + 1) * t_dec, h * Q_LAT_W:(h + 1) * Q_LAT_W] for h in range(H_A)], axis=0)
    qlat = ql[:, :D_C].astype(BF16)
    qrope = ql[:, D_C:D_C + DR_A].astype(BF16)

    ck = jnp.concatenate([r[...].astype(BF16) for r in ckv_refs], axis=0)
    kt = jnp.concatenate([r[...].astype(BF16) for r in kpe_refs], axis=1)
    s = _dot_nt(qlat, ck) + _dot(qrope, kt)

    pad = PAGE_SIZE - t_dec
    cn = jnp.concatenate([ckvn_ref[...], jnp.zeros((pad, D_C), F32)], axis=0).astype(BF16)
    kn = jnp.concatenate([kpen_ref[...][:, :DR_A], jnp.zeros((pad, DR_A), F32)], axis=0).astype(BF16)
    r = lax.broadcasted_iota(jnp.int32, (rows, PAGE_SIZE), 0) % t_dec
    cc = lax.broadcasted_iota(jnp.int32, (rows, PAGE_SIZE), 1)
    s_new = jnp.where(cc <= r, _dot_nt(qlat, cn) + _dot_nt(qrope, kn), NEG)

    m = jnp.maximum(jnp.max(s, -1, keepdims=True), jnp.max(s_new, -1, keepdims=True))
    p = jnp.exp(s - m)
    p_new = jnp.exp(s_new - m)
    l = jnp.sum(p, -1, keepdims=True) + jnp.sum(p_new, -1, keepdims=True)
    acc = _dot(p.astype(BF16), ck) + _dot(p_new.astype(BF16), cn)
    o_lat = (acc / l).astype(BF16)
    full = _dot(o_lat, wuv_ref[...])
    rh = lax.broadcasted_iota(jnp.int32, full.shape, 0) // t_dec
    ch = lax.broadcasted_iota(jnp.int32, full.shape, 1) // DV_A
    full = jnp.where(rh == ch, full, 0.0)
    out = full[0:t_dec]
    for h in range(1, H_A):
        out = out + full[h * t_dec:(h + 1) * t_dec]
    o_ref[...] = out


def _mla_sample(page_table, q_s, ckv_new, kpe_new, wq2, wuv, cache_ckv, cache_kpe_t):
    nb, t_dec = q_s.shape[:2]
    n_pages = page_table.shape[1]

    def page_map(p):
        return lambda b, pt: (0, pt[b * n_pages + p], 0, 0)

    per_b = lambda w: pl.BlockSpec((None, t_dec, w), lambda b, pt: (b, 0, 0))
    grid_spec = pltpu.PrefetchScalarGridSpec(
        num_scalar_prefetch=1,
        grid=(nb,),
        in_specs=[per_b(H_A * LANES), per_b(D_C), per_b(LANES),
                  pl.BlockSpec(wq2.shape, lambda b, pt: (0, 0)),
                  pl.BlockSpec(wuv.shape, lambda b, pt: (0, 0))]
                 + [pl.BlockSpec((None, None, PAGE_SIZE, D_C), page_map(p)) for p in range(n_pages)]
                 + [pl.BlockSpec((None, None, DR_A, PAGE_SIZE), page_map(p)) for p in range(n_pages)],
        out_specs=per_b(H_A * DV_A),
    )
    return pl.pallas_call(
        functools.partial(_mla_sample_kernel, n_pages=n_pages, t_dec=t_dec),
        grid_spec=grid_spec,
        out_shape=jax.ShapeDtypeStruct((nb, t_dec, H_A * DV_A), F32),
        compiler_params=_params(1),
        name="mla_sample",
    )(page_table.reshape(-1), q_s, ckv_new, kpe_new, wq2, wuv,
      *([cache_ckv] * n_pages), *([cache_kpe_t] * n_pages))


def _proj_ln_kernel(x_ref, a_ref, b_ref, wa_ref, wb_ref, g_ref, beta_ref, o_ref):
    att = jnp.concatenate([a_ref[p] for p in range(a_ref.shape[0])], axis=1)
    y = _dot(att, wa_ref[...]) + _dot(b_ref[...], wb_ref[...])
    o_ref[...] = _layer_norm(ALPHA * x_ref[...] + y, g_ref[...], beta_ref[...])


def _proj_ln(x, a, b, wa, wb, g, beta, tm):
    m, d = x.shape
    row = lambda arr: pl.BlockSpec((tm, arr.shape[1]), lambda i: (i, 0))
    return pl.pallas_call(
        _proj_ln_kernel,
        grid=(m // tm,),
        in_specs=[row(x), pl.BlockSpec((a.shape[0], tm, LANES), lambda i: (0, i, 0)), row(b),
                  _full_spec(wa), _full_spec(wb), _full_spec(g), _full_spec(beta)],
        out_specs=pl.BlockSpec((tm, d), lambda i: (i, 0)),
        out_shape=jax.ShapeDtypeStruct((m, d), F32),
        compiler_params=_params(1),
        name="mix_out_ln",
    )(x, a, b, wa, wb, g, beta)


def _ffn_kernel(x_ref, wg_ref, wu_ref, wd_ref, g_ref, beta_ref, o_ref, ob_ref, *, n_chunks):
    x = x_ref[...]
    xb = x.astype(BF16)
    fc = wg_ref.shape[1] // n_chunks
    y = jnp.zeros(x.shape, F32)
    for c in range(n_chunks):
        cs = slice(c * fc, (c + 1) * fc)
        h = jax.nn.silu(_dot(xb, wg_ref[:, cs])) * _dot(xb, wu_ref[:, cs])
        y = y + _dot(h.astype(BF16), wd_ref[cs, :])
    out = _layer_norm(ALPHA * x + y, g_ref[...], beta_ref[...])
    o_ref[...] = out
    ob_ref[...] = out.astype(BF16)


def _ffn(x, wg, wu, wd, g, beta, tm, n_chunks):
    m, d = x.shape
    resident = lambda arr: pl.BlockSpec(arr.shape, lambda i: (0, 0), pipeline_mode=pl.Buffered(1))
    row = pl.BlockSpec((tm, d), lambda i: (i, 0))
    return pl.pallas_call(
        functools.partial(_ffn_kernel, n_chunks=n_chunks),
        grid=(m // tm,),
        in_specs=[row, resident(wg), resident(wu), resident(wd), resident(g), resident(beta)],
        out_specs=[row, row],
        out_shape=[jax.ShapeDtypeStruct((m, d), F32), jax.ShapeDtypeStruct((m, d), BF16)],
        compiler_params=_params(1),
        name="ffn_ln",
    )(x, wg, wu, wd, g, beta)


def _odd_in_prompt_kernel(x_ref, w_ref, q_ref, k_ref, v_ref, kv_ref, scr_ref, *, d, scale, kv_rows):
    tm = x_ref.shape[0]
    n = tm // d
    x = x_ref[...]
    for part, (o_ref, s) in enumerate(((q_ref, scale), (k_ref, 1.0), (v_ref, 1.0))):
        y = _dot(x, w_ref[:, part * W_C:(part + 1) * W_C])
        if part:
            kv_ref[:, (part - 1) * W_C:part * W_C] = y[tm - kv_rows:, :]
        if d == 1:
            o_ref[0] = (y * s).astype(BF16)
        else:
            for c in range(W_C // LANES):
                scr_ref[c] = y[:, c * LANES:(c + 1) * LANES] * s
            for r in range(d):
                o_ref[r] = jnp.concatenate(
                    [scr_ref[c, pl.ds(r, n, stride=d), :] for c in range(W_C // LANES)], axis=1).astype(BF16)


def _odd_in_prompt(xb, w_g, g, n_batch, seq, tm):
    d = C_DILATIONS[g]
    win = min(C_WINDOWS[g], seq)
    tiles = seq // tm
    assert tm % d == 0 and (tm // d) % 16 == 0
    if win >= tm:
        assert win % tm == 0
        kv_rows, per_b, first = tm, win // tm, (seq - win) // tm
    else:
        assert tm % win == 0
        kv_rows, per_b, first = win, 1, tiles - 1
    dil = pl.BlockSpec((None, d, tm // d, W_C), lambda i: (i // tiles, 0, i % tiles, 0))
    kv_spec = pl.BlockSpec((kv_rows, 2 * W_C),
                           lambda i: ((i // tiles) * per_b + jnp.maximum(i % tiles - first, 0), 0))
    dil_shape = jax.ShapeDtypeStruct((n_batch, d, seq // d, W_C), BF16)
    return pl.pallas_call(
        functools.partial(_odd_in_prompt_kernel, d=d, scale=DH_C ** -0.5, kv_rows=kv_rows),
        grid=(n_batch * tiles,),
        in_specs=[pl.BlockSpec((tm, xb.shape[1]), lambda i: (i, 0)), _full_spec(w_g)],
        out_specs=[dil, dil, dil, kv_spec],
        out_shape=[dil_shape, dil_shape, dil_shape, jax.ShapeDtypeStruct((n_batch * win, 2 * W_C), F32)],
        scratch_shapes=[pltpu.VMEM((W_C // LANES, tm, LANES), F32)],
        compiler_params=_params(1),
        name=f"odd_in_prompt_g{g}",
    )(xb, w_g)


def _odd_in_sample_kernel(x_ref, w_ref, q_ref, kv_ref, *, scale):
    y = _dot(x_ref[...], w_ref[...])
    q_ref[...] = y[:, :W_C] * scale
    kv_ref[...] = y[:, W_C:]


def _odd_in_sample(xb, w_g, g, row0, n_rows, tm):
    return pl.pallas_call(
        functools.partial(_odd_in_sample_kernel, scale=DH_C ** -0.5),
        grid=(n_rows // tm,),
        in_specs=[pl.BlockSpec((tm, xb.shape[1]), lambda i: (row0 // tm + i, 0)), _full_spec(w_g)],
        out_specs=[pl.BlockSpec((tm, W_C), lambda i: (i, 0)), pl.BlockSpec((tm, 2 * W_C), lambda i: (i, 0))],
        out_shape=[jax.ShapeDtypeStruct((n_rows, W_C), F32), jax.ShapeDtypeStruct((n_rows, 2 * W_C), F32)],
        compiler_params=_params(1),
        name=f"odd_in_sample_g{g}",
    )(xb, w_g)


def _dsw_prompt_kernel(q_ref, k_ref, v_ref, bias_ref, o_ref, lse_ref, *, n_blk):
    blk = C_BLOCK
    lane = lax.broadcasted_iota(jnp.int32, (blk, LANES), 1)
    lo = lane < DH_C

    def block(r, n, first):
        if first:
            qs = ks = pl.ds(0, blk)
        else:
            qs = pl.ds(pl.multiple_of(n * blk, blk), blk)
            ks = pl.ds(pl.multiple_of((n - 1) * blk, blk), 2 * blk)
        lse_tile = jnp.zeros((blk, LANES), F32)
        for hp in range(H_C // 2):
            hs = slice(hp * LANES, (hp + 1) * LANES)
            qp = q_ref[r, qs, hs]
            kp = k_ref[r, ks, hs]
            vp = v_ref[r, ks, hs]
            outs = []
            for j in range(2):
                h = 2 * hp + j
                qm = jnp.where(lo if j == 0 else ~lo, qp, jnp.zeros_like(qp))
                bias = bias_ref[h][:, blk:] if first else bias_ref[h]
                s = _dot_nt(qm, kp) + bias
                m = jnp.max(s, -1, keepdims=True)
                p = jnp.exp(s - m)
                l = jnp.sum(p, -1, keepdims=True)
                outs.append(_dot((p / l).astype(BF16), vp))
                lse_tile = jnp.where(lane == h, m + jnp.log(l), lse_tile)
            o_ref[r, qs, hs] = jnp.where(lo, outs[0], outs[1]).astype(BF16)
        lse_ref[r, qs, :] = lse_tile

    def residue(r, carry):
        block(r, 0, True)
        if n_blk > 1:
            def body(n, c):
                block(r, n, False)
                return c
            lax.fori_loop(1, n_blk, body, 0)
        return carry

    lax.fori_loop(0, q_ref.shape[0], residue, 0)


def _dsw_prompt(q, k, v, bias, g):
    n_batch, d, length, w = q.shape
    assert length % C_BLOCK == 0
    spec = lambda width: pl.BlockSpec((None, d, length, width), lambda b: (b, 0, 0, 0))
    return pl.pallas_call(
        functools.partial(_dsw_prompt_kernel, n_blk=length // C_BLOCK),
        grid=(n_batch,),
        in_specs=[spec(w), spec(w), spec(w), _full_spec(bias)],
        out_specs=[spec(w), spec(LANES)],
        out_shape=[jax.ShapeDtypeStruct((n_batch, d, length, w), BF16),
                   jax.ShapeDtypeStruct((n_batch, d, length, LANES), F32)],
        compiler_params=_params(1),
        name=f"dsw_prompt_g{g}",
    )(q, k, v, bias)


def _dsw_sample_kernel(q_ref, qh_ref, kvn_ref, buf_ref, mask_ref, maskn_ref, o_ref, lse_ref, *, t_dec):
    rows = H_C * t_dec
    rh = lax.broadcasted_iota(jnp.int32, (rows, W_C), 0) // t_dec
    ch = lax.broadcasted_iota(jnp.int32, (rows, W_C), 1) // DH_C
    same = rh == ch
    lane = lax.broadcasted_iota(jnp.int32, (rows, LANES), 1)
    rl = lax.broadcasted_iota(jnp.int32, (rows, LANES), 0) // t_dec
    pad = LANES - t_dec

    q = q_ref[...]
    qbd = jnp.where(same, jnp.concatenate([q] * H_C, axis=0), 0.0).astype(BF16)
    kvn = kvn_ref[...]
    kn = jnp.concatenate([kvn[:, :W_C], jnp.zeros((pad, W_C), F32)], axis=0).astype(BF16)
    vn = jnp.concatenate([kvn[:, W_C:], jnp.zeros((pad, W_C), F32)], axis=0).astype(BF16)
    s_n = _dot_nt(qbd, kn) + maskn_ref[...]

    s_b = jnp.concatenate(
        [_dot(qh_ref[h].astype(BF16), buf_ref[0, h].astype(BF16)) for h in range(H_C)], axis=0) + mask_ref[...]

    m = jnp.maximum(jnp.max(s_b, -1, keepdims=True), jnp.max(s_n, -1, keepdims=True))
    p_b = jnp.exp(s_b - m)
    p_n = jnp.exp(s_n - m)
    l = jnp.sum(p_b, -1, keepdims=True) + jnp.sum(p_n, -1, keepdims=True)
    p_b = p_b / l
    o_n = jnp.where(same, _dot((p_n / l).astype(BF16), vn), 0.0)
    lse = jnp.where(lane == rl, m + jnp.log(l), 0.0)
    o_acc = o_n[0:t_dec]
    lse_acc = lse[0:t_dec]
    for h in range(1, H_C):
        o_acc = o_acc + o_n[h * t_dec:(h + 1) * t_dec]
        lse_acc = lse_acc + lse[h * t_dec:(h + 1) * t_dec]
    o_b = jnp.concatenate(
        [_dot_nt(p_b[h * t_dec:(h + 1) * t_dec].astype(BF16), buf_ref[1, h].astype(BF16)) for h in range(H_C)],
        axis=1)
    o_ref[...] = o_acc + o_b
    lse_ref[...] = lse_acc


def _dsw_sample(q_s, q_hm, kv_new, buf_t, mask, mask_new):
    nb, _, t_dec, _ = q_hm.shape
    ms = nb * t_dec
    row = lambda width: pl.BlockSpec((t_dec, width), lambda b: (b, 0))
    return pl.pallas_call(
        functools.partial(_dsw_sample_kernel, t_dec=t_dec),
        grid=(nb,),
        in_specs=[row(W_C), pl.BlockSpec((None,) + q_hm.shape[1:], lambda b: (b, 0, 0, 0)), row(2 * W_C),
                  pl.BlockSpec((None,) + buf_t.shape[1:], lambda b: (b, 0, 0, 0, 0)),
                  _full_spec(mask), _full_spec(mask_new)],
        out_specs=[row(W_C), row(LANES)],
        out_shape=[jax.ShapeDtypeStruct((ms, W_C), F32), jax.ShapeDtypeStruct((ms, LANES), F32)],
        compiler_params=_params(1),
        name="dsw_sample",
    )(q_s, q_hm, kv_new, buf_t, mask, mask_new)


def _split_bf16(x):
    hi = x.astype(BF16)
    return hi, (x - hi.astype(F32)).astype(BF16)


def _interleave(ref, scr_ref):
    d, n, width = ref.shape
    if d == 1:
        return ref[0].astype(F32)
    slabs = width // LANES
    for r in range(d):
        val = ref[r].astype(F32)
        for c in range(slabs):
            scr_ref[c, pl.ds(r, n, stride=d), :] = val[:, c * LANES:(c + 1) * LANES]
    return jnp.concatenate([scr_ref[c] for c in range(slabs)], axis=1)


def _merge_kernel(x_ref, o0_ref, o1_ref, o2_ref, l0_ref, l1_ref, l2_ref, e_ref, w_ref, g_ref, beta_ref,
                  wr_hi_ref, wr_lo_ref, *rest, n_alias):
    y_ref, yb_ref, route_ref, o_scr, l_scr = rest[n_alias:]
    lses = [_interleave(l_ref, l_scr) for l_ref in (l0_ref, l1_ref, l2_ref)]
    mx = jnp.maximum(jnp.maximum(lses[0], lses[1]), lses[2])
    es = [jnp.exp(l - mx) for l in lses]
    den = es[0] + es[1] + es[2]
    acc = None
    for e, o_ref in zip(es, (o0_ref, o1_ref, o2_ref)):
        hi, lo = _split_bf16(e / den)
        wexp = _dot(hi, e_ref[...]) + _dot(lo, e_ref[...])
        term = wexp * _interleave(o_ref, o_scr)
        acc = term if acc is None else acc + term
    y = _dot(acc.astype(BF16), w_ref[...])
    x2 = _layer_norm(ALPHA * x_ref[...] + y, g_ref[...], beta_ref[...])
    y_ref[...] = x2
    yb_ref[...] = x2.astype(BF16)

    xh, xl = _split_bf16(x2)
    logits = _dot(xh, wr_hi_ref[...]) + _dot(xl, wr_hi_ref[...]) + _dot(xh, wr_lo_ref[...])
    lane = lax.broadcasted_iota(jnp.int32, logits.shape, 1)
    logits = jnp.where(lane < N_EXPERTS, logits, -jnp.inf)
    v1 = jnp.max(logits, -1, keepdims=True)
    i1 = jnp.min(jnp.where(logits == v1, lane, LANES), -1, keepdims=True)
    rest_l = jnp.where(lane == i1, -jnp.inf, logits)
    v2 = jnp.max(rest_l, -1, keepdims=True)
    i2 = jnp.min(jnp.where(rest_l == v2, lane, LANES), -1, keepdims=True)
    e2 = jnp.exp(v2 - v1)
    g1 = 1.0 / (1.0 + e2)
    g2 = e2 / (1.0 + e2)
    route = jnp.where(lane == 0, i1.astype(F32), 0.0)
    route = jnp.where(lane == 1, i2.astype(F32), route)
    route = jnp.where(lane == 2, g1, route)
    route = jnp.where(lane == 3, g2, route)
    route_ref[...] = route


def _merge(x, os_, lses, weights, tm, row0, n_rows, tiles_per_batch, prev):
    m, d_model = x.shape
    base = row0 // tm

    def dil(arr):
        _, d, _, width = arr.shape
        return pl.BlockSpec((None, d, tm // d, width),
                            lambda i: (i // tiles_per_batch, 0, i % tiles_per_batch, 0))

    row = lambda width: pl.BlockSpec((tm, width), lambda i: (base + i, 0))
    outs = [jax.ShapeDtypeStruct((m, d_model), F32), jax.ShapeDtypeStruct((m, d_model), BF16),
            jax.ShapeDtypeStruct((m, LANES), F32)]
    n_alias = 0 if prev is None else len(prev)
    n_in = 7 + len(weights)
    return pl.pallas_call(
        functools.partial(_merge_kernel, n_alias=n_alias),
        grid=(n_rows // tm,),
        in_specs=[row(d_model)] + [dil(o) for o in os_] + [dil(l) for l in lses]
                 + [_full_spec(a) for a in weights] + [pl.BlockSpec(memory_space=pl.ANY)] * n_alias,
        out_specs=[row(d_model), row(d_model), row(LANES)],
        out_shape=outs,
        scratch_shapes=[pltpu.VMEM((W_C // LANES, tm, LANES), F32), pltpu.VMEM((1, tm, LANES), F32)],
        input_output_aliases={n_in + j: j for j in range(n_alias)},
        compiler_params=_params(1),
        name="merge_out_ln_route",
    )(x, *os_, *lses, *weights, *(prev or ()))


def _moe_kernel(te_ref, nt_ref, x_ref, wg_ref, wu_ref, wd_ref, o_ref, *, n_chunks):
    t = pl.program_id(0)

    @pl.when(t < nt_ref[0])
    def _():
        xb = x_ref[...]
        fc = wg_ref.shape[1] // n_chunks
        y = jnp.zeros(o_ref.shape, F32)
        for c in range(n_chunks):
            cs = slice(c * fc, (c + 1) * fc)
            h = jax.nn.silu(_dot(xb, wg_ref[:, cs])) * _dot(xb, wu_ref[:, cs])
            y = y + _dot(h.astype(BF16), wd_ref[cs, :])
        o_ref[...] = y

    @pl.when(t >= nt_ref[0])
    def _():
        o_ref[...] = jnp.zeros(o_ref.shape, F32)


def _moe(tile_expert, n_tiles_used, xs, wg, wu, wd, tm, n_chunks):
    p, d = xs.shape
    f = wg.shape[2]
    grid_spec = pltpu.PrefetchScalarGridSpec(
        num_scalar_prefetch=2,
        grid=(p // tm,),
        in_specs=[pl.BlockSpec((tm, d), lambda t, te, nt: (t, 0)),
                  pl.BlockSpec((None, d, f), lambda t, te, nt: (te[t], 0, 0)),
                  pl.BlockSpec((None, d, f), lambda t, te, nt: (te[t], 0, 0)),
                  pl.BlockSpec((None, f, d), lambda t, te, nt: (te[t], 0, 0))],
        out_specs=pl.BlockSpec((tm, d), lambda t, te, nt: (t, 0)),
    )
    return pl.pallas_call(
        functools.partial(_moe_kernel, n_chunks=n_chunks),
        grid_spec=grid_spec,
        out_shape=jax.ShapeDtypeStruct((p, d), F32),
        compiler_params=_params(1),
        name="moe_experts",
    )(tile_expert, n_tiles_used, xs, wg, wu, wd)


def _final_kernel(x_ref, ya_ref, yb_ref, route_ref, g_ref, beta_ref, o_ref):
    route = route_ref[...]
    y = route[:, 2:3] * ya_ref[...] + route[:, 3:4] * yb_ref[...]
    o_ref[...] = _layer_norm(ALPHA * x_ref[...] + y, g_ref[...], beta_ref[...])


def _final(x, ya, yb, route, g, beta, tm):
    m, d = x.shape
    row = lambda arr: pl.BlockSpec((tm, arr.shape[1]), lambda i: (i, 0))
    return pl.pallas_call(
        _final_kernel,
        grid=(m // tm,),
        in_specs=[row(x), row(ya), row(yb), row(route), _full_spec(g), _full_spec(beta)],
        out_specs=row(x),
        out_shape=jax.ShapeDtypeStruct((m, d), F32),
        compiler_params=_params(1),
        name="moe_combine_ln",
    )(x, ya, yb, route, g, beta)


def _rope_table(pos, scale):
    half = DR_A // 2
    inv = ROPE_BASE ** (-jnp.arange(half, dtype=F32) / half)
    ang = pos.astype(F32)[:, None] * inv[None, :]
    cos, sin = jnp.cos(ang), jnp.sin(ang)
    n = pos.shape[0]
    z = lambda w: jnp.zeros((n, w), F32)
    one = jnp.ones((n, DN_A), F32)
    cq = jnp.concatenate([one, cos, cos, z(32)], 1) * scale
    s1q = jnp.concatenate([z(DN_A), -sin, z(16), z(32)], 1) * scale
    s2q = jnp.concatenate([z(DN_A), z(16), sin, z(32)], 1) * scale
    ck = jnp.concatenate([cos, cos, z(96)], 1)
    s1k = jnp.concatenate([-sin, z(112)], 1)
    s2k = jnp.concatenate([z(16), sin, z(96)], 1)
    return jnp.concatenate([cq, s1q, s2q, ck, s1k, s2k], 1)


def _t5_bucket(dist):
    max_exact = N_BUCKETS // 2
    log_ratio = jnp.log(jnp.maximum(dist, 1).astype(F32) / max_exact) / math.log(MAX_DISTANCE / max_exact)
    large = jnp.minimum(max_exact + (log_ratio * (N_BUCKETS - max_exact)).astype(jnp.int32), N_BUCKETS - 1)
    return jnp.where(dist < max_exact, dist, large)


def _group_bias(rel_bias, g):
    d = C_DILATIONS[g]
    dist = d * jnp.arange(C_WINDOWS[g] // d + 1, dtype=jnp.int32)
    onehot = (_t5_bucket(dist)[:, None] == jnp.arange(N_BUCKETS)[None, :]).astype(F32)
    table = rel_bias[:, g * H_C:(g + 1) * H_C]
    return jnp.transpose(jnp.dot(onehot, table, precision=lax.Precision.HIGHEST))


def _prompt_bias(bias):
    n_back = bias.shape[1] - 1
    a = np.arange(C_BLOCK)[:, None]
    c = np.arange(2 * C_BLOCK)[None, :]
    j = a + C_BLOCK - c
    band = (j >= 0) & (j <= n_back)
    full = _select_columns(bias, np.clip(j, 0, n_back).reshape(-1))
    return jnp.where(band[None], full.reshape(-1, C_BLOCK, 2 * C_BLOCK), NEG)


def _select_columns(bias, cols):
    sel = (jnp.arange(bias.shape[1], dtype=jnp.int32)[:, None] == jnp.asarray(cols, jnp.int32)[None, :])
    return jnp.dot(bias.astype(F32), sel.astype(F32), precision=lax.Precision.HIGHEST)


def _sample_masks(bias, g, n_buf, t_dec):
    d = C_DILATIONS[g]
    n_back = bias.shape[1] - 1
    t = np.arange(t_dec)[:, None]

    def build(key_pos):
        diff = n_buf + t - key_pos[None, :]
        ok = (diff >= 0) & (diff % d == 0) & (diff // d <= n_back)
        full = _select_columns(bias, np.clip(diff // d, 0, n_back).reshape(-1))
        mt = jnp.where(ok[None], full.reshape(H_C, t_dec, -1), NEG)
        return mt.reshape(H_C * t_dec, -1)

    mask_new = build(n_buf + np.arange(LANES))
    col_ok = (np.arange(LANES) < t_dec)[None, :]
    return build(np.arange(n_buf)), jnp.where(col_ok, mask_new, NEG)


def _prep_even(e_w_in, e_q_norm_g, e_kv_norm_g, e_w_uq, e_w_uk, e_w_uv, e_v_ln_g, e_v_ln_b, e_w_s, e_b_s, t_dec):
    o_kr = D_CQ + D_C
    o_uv = o_kr + DR_A
    d_model = e_w_in.shape[0]
    w_in = jnp.concatenate([e_w_in[:, :o_kr], e_w_in[:, o_uv:], e_w_in[:, o_kr:o_uv],
                            jnp.zeros((d_model, LANES - DR_A), F32)], axis=1).astype(BF16)
    w_uq = jnp.pad(e_w_uq, ((0, 0), (0, 0), (0, LANES - DN_A - DR_A))).reshape(D_CQ, H_A * LANES).astype(BF16)
    w_uk_pad = jnp.pad(e_w_uk, ((0, 0), (0, 0), (0, LANES - DN_A))).reshape(D_C, H_A * LANES)
    w_uv = e_w_uv.reshape(D_C, H_A * DV_A)
    place = np.zeros((LANES, H_A, LANES), np.float32)
    for i in range(DR_A):
        place[i, :, DN_A + i] = 1.0
    place = place.reshape(LANES, H_A * LANES)
    w_kv = jnp.concatenate([
        jnp.concatenate([w_uk_pad, w_uv], axis=1),
        jnp.concatenate([jnp.asarray(place), jnp.zeros((LANES, H_A * DV_A), F32)], axis=1)], axis=0).astype(BF16)

    tril = np.tril(np.ones((CHUNK, CHUNK), bool))
    wg_p = jnp.where(tril[None], e_w_s, 0.0)
    small = jnp.where(np.tril(np.ones((t_dec, t_dec), bool))[None], e_w_s[:, :t_dec, :t_dec], 0.0)
    eye = jnp.eye(CHUNK // t_dec, dtype=F32)
    wg_s = jnp.einsum("ab,gts->gatbs", eye, small).reshape(G_B, CHUNK, CHUNK)
    wgate = jnp.stack([wg_p, wg_s]).astype(BF16)
    bg_p = jnp.broadcast_to(e_b_s[:, :, None], (G_B, CHUNK, LANES))
    bg_s = jnp.broadcast_to(jnp.tile(e_b_s[:, :t_dec], (1, CHUNK // t_dec))[:, :, None], (G_B, CHUNK, LANES))
    bgate = jnp.stack([bg_p, bg_s]).astype(F32)

    sel = np.zeros((LANES, Q_LAT_W), np.float32)
    for i in range(DR_A):
        sel[DN_A + i, D_C + i] = 1.0
    wq2 = jnp.broadcast_to(jnp.asarray(sel)[:, None, :], (LANES, H_A, Q_LAT_W))
    uk_t = jnp.transpose(e_w_uk, (2, 1, 0))
    wq2 = wq2.at[:DN_A, :, :D_C].add(uk_t)
    wq2 = wq2.reshape(LANES, H_A * Q_LAT_W).astype(BF16)
    return dict(w_in=w_in, gq=e_q_norm_g.reshape(1, -1), gkv=e_kv_norm_g.reshape(1, -1), w_uq=w_uq, w_kv=w_kv,
                vg=e_v_ln_g.reshape(1, -1), vb=e_v_ln_b.reshape(1, -1), wgate=wgate, bgate=bgate,
                wq2=wq2, wuv=w_uv.astype(BF16))


def _pick_tile(m_prompt, m_sample, cap):
    t = cap
    while m_prompt % t or m_sample % t:
        t //= 2
    assert t >= CHUNK
    return t


def kernel(x_prompt, x_sample, cache_mla_ckv, cache_mla_kpe, cache_dsw_kv0, cache_dsw_kv1, cache_dsw_kv2, page_table, ln_g, ln_b, e_w_in, e_q_norm_g, e_kv_norm_g, e_w_uq, e_w_uk, e_w_uv, e_v_ln_g, e_v_ln_b, e_w_s, e_b_s, e_w_out, ffn_w_gate, ffn_w_up, ffn_w_down, o_w_in, o_w_out, rel_bias, moe_w_router, moe_w_gate, moe_w_up, moe_w_down):
    nb_p, seq, d_model = x_prompt.shape
    nb_s, t_dec, _ = x_sample.shape
    past_len = page_table.shape[1] * PAGE_SIZE
    mp, ms = nb_p * seq, nb_s * t_dec
    m = mp + ms
    tm = _pick_tile(seq, ms, 512)
    tm_p = _pick_tile(seq, seq, 512)
    tm_s = _pick_tile(ms, ms, 512)
    assert CHUNK % t_dec == 0 and seq % CHUNK == 0 and mp % tm_s == 0
    ffn_chunks = 4

    x0 = jnp.concatenate([x_prompt.reshape(mp, d_model), x_sample.reshape(ms, d_model)], axis=0)
    ln = lambda l, j: (ln_g[l, j].reshape(1, -1), ln_b[l, j].reshape(1, -1))

    prep = _prep_even(e_w_in[0], e_q_norm_g[0], e_kv_norm_g[0], e_w_uq[0], e_w_uk[0], e_w_uv[0], e_v_ln_g[0],
                      e_v_ln_b[0], e_w_s[0], e_b_s[0], t_dec)
    scale = (DN_A + DR_A) ** -0.5
    pos = jnp.concatenate([jnp.arange(seq, dtype=jnp.int32),
                           past_len + (jnp.arange(tm, dtype=jnp.int32) % t_dec)])
    tab = _rope_table(pos, scale)
    q, k, v, ckv, kpe, gated, vln = _even_in(x0, prep, tab, tm, mp // tm, seq // tm)

    att = _mla_prompt(q, k, v, nb_p, seq, min(256, seq))
    q_s = jnp.transpose(q[:, mp:].astype(F32), (1, 0, 2)).reshape(nb_s, t_dec, H_A * LANES)
    ckv_s = ckv[mp:].reshape(nb_s, t_dec, D_C)
    kpe_s = kpe[mp:].reshape(nb_s, t_dec, LANES)
    att_s = _mla_sample(page_table, q_s, ckv_s, kpe_s, prep["wq2"], prep["wuv"], cache_mla_ckv,
                        jnp.swapaxes(cache_mla_kpe, 2, 3))
    att_s = jnp.transpose(att_s.reshape(ms, H_A // 2, LANES), (1, 0, 2)).astype(BF16)
    att = lax.dynamic_update_slice(att, att_s, (0, mp, 0))

    w_out_e = e_w_out[0].astype(BF16)
    n_att = H_A * DV_A
    x1 = _proj_ln(x0, att, gated, w_out_e[:n_att], w_out_e[n_att:], *ln(0, 0), tm)
    x1, x1b = _ffn(x1, ffn_w_gate[0].astype(BF16), ffn_w_up[0].astype(BF16), ffn_w_down[0].astype(BF16),
                   *ln(0, 1), tm, ffn_chunks)

    ng = N_GROUPS_C
    w_in_o = o_w_in[0].reshape(d_model, ng, 3 * W_C).astype(BF16)
    caches = (cache_dsw_kv0, cache_dsw_kv1, cache_dsw_kv2)
    os_p, lses_p, os_s, lses_s, dsw_p, dsw_s = [], [], [], [], [], []
    for g in range(ng):
        w_g = w_in_o[:, g]
        bias = _group_bias(rel_bias, g)
        qd, kd, vd, kv_tail = _odd_in_prompt(x1b, w_g, g, nb_p, seq, tm_p)
        win = min(C_WINDOWS[g], seq)
        dsw_p.append(kv_tail.reshape(1, nb_p, win, 2, H_C, DH_C))
        o_g, lse_g = _dsw_prompt(qd, kd, vd, _prompt_bias(bias), g)
        os_p.append(o_g)
        lses_p.append(lse_g)

        q_new, kv_new = _odd_in_sample(x1b, w_g, g, mp, ms, tm_s)
        dsw_s.append(kv_new.reshape(1, nb_s, t_dec, 2, H_C, DH_C))
        n_buf = caches[g].shape[2]
        buf_t = jnp.transpose(caches[g][0], (0, 2, 3, 4, 1))
        q_hm = jnp.transpose(q_new.reshape(nb_s, t_dec, H_C, DH_C), (0, 2, 1, 3))
        mask, mask_new = _sample_masks(bias, g, n_buf, t_dec)
        o_g, lse_g = _dsw_sample(q_new, q_hm, kv_new, buf_t, mask, mask_new)
        os_s.append(o_g.reshape(1, 1, ms, W_C))
        lses_s.append(lse_g.reshape(1, 1, ms, LANES))

    expand = np.zeros((LANES, W_C), np.float32)
    for h in range(H_C):
        expand[h, h * DH_C:(h + 1) * DH_C] = 1.0
    wr = jnp.pad(moe_w_router[0], ((0, 0), (0, LANES - N_EXPERTS)))
    wr_hi, wr_lo = _split_bf16(wr)
    weights = [jnp.asarray(expand, BF16), o_w_out[0].astype(BF16), *ln(1, 0), wr_hi, wr_lo]
    merged = _merge(x1, os_p, lses_p, weights, tm_p, 0, mp, seq // tm_p, None)
    x2, x2b, route = _merge(x1, os_s, lses_s, weights, tm_s, mp, ms, ms // tm_s, merged)

    tmoe = 256
    idx = route[:, :TOP_K].astype(jnp.int32)
    flat_e = idx.reshape(-1)
    onehot = (flat_e[:, None] == jnp.arange(N_EXPERTS)[None, :]).astype(jnp.int32)
    csum = jnp.cumsum(onehot, axis=0)
    counts = csum[-1]
    rank = jnp.sum(csum * onehot, axis=1) - 1
    padded = ((counts + tmoe - 1) // tmoe) * tmoe
    ends = jnp.cumsum(padded)
    starts = ends - padded
    pos_sorted = jnp.sum(starts[None, :] * onehot, axis=1) + rank
    p_rows = ((TOP_K * m + tmoe - 1) // tmoe + N_EXPERTS) * tmoe
    n_tiles = p_rows // tmoe
    src = jnp.zeros((p_rows,), jnp.int32).at[pos_sorted].set(
        jnp.arange(TOP_K * m, dtype=jnp.int32) // TOP_K, unique_indices=True, mode="promise_in_bounds")
    xs = x2b.at[src].get(mode="promise_in_bounds")
    tile_start = jnp.arange(n_tiles, dtype=jnp.int32) * tmoe
    tile_expert = jnp.minimum(jnp.sum(tile_start[:, None] >= ends[None, :], axis=1), N_EXPERTS - 1).astype(jnp.int32)
    n_used = (ends[-1] // tmoe).astype(jnp.int32).reshape(1)
    ys = _moe(tile_expert, n_used, xs, moe_w_gate[0].astype(BF16), moe_w_up[0].astype(BF16),
              moe_w_down[0].astype(BF16), tmoe, ffn_chunks)
    pos2 = pos_sorted.reshape(m, TOP_K)
    ya = ys.at[pos2[:, 0]].get(mode="promise_in_bounds")
    yb = ys.at[pos2[:, 1]].get(mode="promise_in_bounds")
    y = _final(x2, ya, yb, route, *ln(1, 1), tm)

    y_prompt = y[:mp].reshape(nb_p, seq, d_model)
    y_sample = y[mp:].reshape(nb_s, t_dec, d_model)
    n_pg = seq // PAGE_SIZE
    new_ckv_prompt = ckv[:mp].reshape(1, nb_p, n_pg, PAGE_SIZE, D_C)
    new_kpe_prompt = kpe[:mp, :DR_A].reshape(1, nb_p, n_pg, PAGE_SIZE, DR_A)
    new_ckv_sample = ckv_s.reshape(1, nb_s, t_dec, D_C)
    new_kpe_sample = kpe_s[:, :, :DR_A].reshape(1, nb_s, t_dec, DR_A)
    new_gate_v_sample = vln[mp:].reshape(1, nb_s, t_dec, D_B)
    return (y_prompt, y_sample, new_ckv_prompt, new_kpe_prompt, new_ckv_sample, new_kpe_sample, new_gate_v_sample,
            dsw_p[0], dsw_p[1], dsw_p[2], dsw_s[0], dsw_s[1], dsw_s[2])
```

```python
import functools
import math

import numpy as np
import jax
import jax.numpy as jnp
from jax import lax
from jax.experimental import pallas as pl
from jax.experimental.pallas import tpu as pltpu

BF16 = jnp.bfloat16
F32 = jnp.float32

H_A, DN_A, DR_A, DV_A = 8, 64, 32, 64
D_CQ, D_C = 384, 256
ROPE_BASE = 10000.0
CHUNK, G_B, D_B = 128, 4, 512
C_WINDOWS = (128, 512, 2048)
C_DILATIONS = (1, 4, 16)
N_GROUPS_C = 3
H_C, DH_C = 16, 64
W_C = H_C * DH_C
C_BLOCK = 128
N_BUCKETS, MAX_DISTANCE = 32, 2048
N_EXPERTS, TOP_K = 8, 2
DEPTH = 2
ALPHA = (2 * DEPTH) ** 0.25
LN_EPS = 1e-5
RMS_EPS = 1e-6
PAGE_SIZE = 128
NEG = -1e30

LANES = 128
V7X_VMEM_BYTES = 64 * 2**20
VMEM_LIMIT = 56 * 2**20


def _dot(a, b):
    return jnp.dot(a, b, preferred_element_type=F32)


def _dot_nt(a, b):
    return lax.dot_general(a, b, (((1,), (1,)), ((), ())), preferred_element_type=F32)


def _params(n_axes, vmem=VMEM_LIMIT):
    return pltpu.CompilerParams(dimension_semantics=("arbitrary",) * n_axes, vmem_limit_bytes=vmem)


def _layer_norm(x, g, b):
    mu = jnp.mean(x, -1, keepdims=True)
    xc = x - mu
    var = jnp.mean(xc * xc, -1, keepdims=True)
    return xc * lax.rsqrt(var + LN_EPS) * g + b


def _rms_norm(x, g):
    return x * lax.rsqrt(jnp.mean(x * x, -1, keepdims=True) + RMS_EPS) * g


def _full_spec(a):
    return pl.BlockSpec(a.shape, lambda *_: (0,) * a.ndim)


def _rope_apply(x, c, s1, s2):
    return x * c + pltpu.roll(x, LANES - 16, 1) * s1 + pltpu.roll(x, 16, 1) * s2


def _even_in_kernel(x_ref, w_in_ref, gq_ref, gkv_ref, w_uq_ref, w_kv_ref, vg_ref, vb_ref, tab_ref,
                    wgate_ref, bgate_ref,
                    q_ref, k_ref, v_ref, ckv_ref, kpe_ref, gated_ref, vln_ref):
    tm = x_ref.shape[0]
    h = _dot(x_ref[...].astype(BF16), w_in_ref[...])
    c_q = h[:, :D_CQ]
    c_kv = h[:, D_CQ:D_CQ + D_C]
    o_u = D_CQ + D_C
    u = jax.nn.gelu(h[:, o_u:o_u + D_B])
    vv = jax.nn.gelu(h[:, o_u + D_B:o_u + 2 * D_B])
    kr = h[:, o_u + 2 * D_B:]

    tab = tab_ref[...]
    cq, s1q, s2q = tab[:, 0:128], tab[:, 128:256], tab[:, 256:384]
    ck, s1k, s2k = tab[:, 384:512], tab[:, 512:640], tab[:, 640:768]

    q = _dot(_rms_norm(c_q, gq_ref[...]).astype(BF16), w_uq_ref[...])
    for hb in range(H_A):
        q_ref[hb] = _rope_apply(q[:, hb * LANES:(hb + 1) * LANES], cq, s1q, s2q).astype(BF16)

    ckv = _rms_norm(c_kv, gkv_ref[...])
    ckv_ref[...] = ckv
    kpe = _rope_apply(kr, ck, s1k, s2k)
    kpe_ref[...] = kpe
    kv = _dot(jnp.concatenate([ckv.astype(BF16), kpe.astype(BF16)], axis=1), w_kv_ref[...])
    for hb in range(H_A):
        k_ref[hb] = kv[:, hb * LANES:(hb + 1) * LANES].astype(BF16)
    for p in range(H_A // 2):
        v_ref[p] = kv[:, (H_A + p) * LANES:(H_A + p + 1) * LANES].astype(BF16)

    vln = _layer_norm(vv, vg_ref[...], vb_ref[...])
    vln_ref[...] = vln
    vb16 = vln.astype(BF16)
    gw = D_B // G_B
    for c in range(tm // CHUNK):
        rs = slice(c * CHUNK, (c + 1) * CHUNK)
        for g in range(G_B):
            cs = slice(g * gw, (g + 1) * gw)
            f = _dot(wgate_ref[g], vb16[rs, cs]) + bgate_ref[g]
            gated_ref[rs, cs] = (u[rs, cs] * f).astype(BF16)


def _even_in(x, prep, tab, tm, n_prompt_tiles, tab_blocks):
    m, d_model = x.shape

    def tab_map(i):
        return (jnp.where(i < n_prompt_tiles, i % tab_blocks, tab_blocks), 0)

    def gate_map(i):
        return (jnp.where(i < n_prompt_tiles, 0, 1), 0, 0, 0)

    row = lambda w: pl.BlockSpec((tm, w), lambda i: (i, 0))
    heads = lambda n: pl.BlockSpec((n, tm, LANES), lambda i: (0, i, 0))
    outs = [
        jax.ShapeDtypeStruct((H_A, m, LANES), BF16),
        jax.ShapeDtypeStruct((H_A, m, LANES), BF16),
        jax.ShapeDtypeStruct((H_A // 2, m, LANES), BF16),
        jax.ShapeDtypeStruct((m, D_C), F32),
        jax.ShapeDtypeStruct((m, LANES), F32),
        jax.ShapeDtypeStruct((m, D_B), BF16),
        jax.ShapeDtypeStruct((m, D_B), F32),
    ]
    weights = [prep[n] for n in ("w_in", "gq", "gkv", "w_uq", "w_kv", "vg", "vb")]
    return pl.pallas_call(
        _even_in_kernel,
        grid=(m // tm,),
        in_specs=[row(d_model)] + [_full_spec(a) for a in weights]
                 + [pl.BlockSpec((tm, 6 * LANES), tab_map),
                    pl.BlockSpec((None, G_B, CHUNK, CHUNK), gate_map),
                    pl.BlockSpec((None, G_B, CHUNK, LANES), gate_map)],
        out_specs=[heads(H_A), heads(H_A), heads(H_A // 2), row(D_C), row(LANES), row(D_B), row(D_B)],
        out_shape=outs,
        compiler_params=_params(1),
        name="even_in",
    )(x, *weights, tab, prep["wgate"], prep["bgate"])


def _mla_prompt_kernel(q_ref, k_ref, v_ref, o_ref, *, tq):
    seq = q_ref.shape[0]
    h = pl.program_id(1)
    lane = lax.broadcasted_iota(jnp.int32, (tq, LANES), 1)
    row = lax.broadcasted_iota(jnp.int32, (tq, tq), 0)
    col = lax.broadcasted_iota(jnp.int32, (tq, tq), 1)
    diag_mask = col <= row
    for i in range(seq // tq):
        cur = slice(i * tq, (i + 1) * tq)
        past = slice(0, i * tq)
        qi = q_ref[cur, :]
        s_d = jnp.where(diag_mask, _dot_nt(qi, k_ref[cur, :]), NEG)
        m = jnp.max(s_d, -1, keepdims=True)
        if i:
            s_p = _dot_nt(qi, k_ref[past, :])
            m = jnp.maximum(m, jnp.max(s_p, -1, keepdims=True))
        p_d = jnp.exp(s_d - m)
        l = jnp.sum(p_d, -1, keepdims=True)
        acc = _dot(p_d.astype(BF16), v_ref[cur, :])
        if i:
            p_p = jnp.exp(s_p - m)
            l = l + jnp.sum(p_p, -1, keepdims=True)
            acc = acc + _dot(p_p.astype(BF16), v_ref[past, :])
        o = (acc / l).astype(BF16)

        @pl.when(h % 2 == 0)
        def _():
            o_ref[cur, :] = o

        @pl.when(h % 2 == 1)
        def _():
            o_ref[cur, :] = jnp.where(lane < DV_A, o_ref[cur, :], o)


def _mla_prompt(q, k, v, n_batch, seq, tq):
    m = q.shape[1]
    per_head = lambda div: pl.BlockSpec((None, seq, LANES), lambda b, h: (h // div, b, 0))
    return pl.pallas_call(
        functools.partial(_mla_prompt_kernel, tq=tq),
        grid=(n_batch, H_A),
        in_specs=[per_head(1), per_head(1), per_head(2)],
        out_specs=per_head(2),
        out_shape=jax.ShapeDtypeStruct((H_A // 2, m, LANES), BF16),
        compiler_params=_params(2),
        name="mla_prompt",
    )(q, k, v)


Q_LAT_W = D_C + LANES


def _mla_sample_kernel(pt_ref, q_ref, ckvn_ref, kpen_ref, wq_ref, wuv_ref, *rest, n_pages, n_chunks, t_dec):
    del pt_ref
    ckv_refs = rest[:n_pages]
    kpe_refs = rest[n_pages:2 * n_pages]
    o_ref = rest[2 * n_pages]
    rows = H_A * t_dec

    q = q_ref[...]
    qs = jnp.concatenate([q[:, h * LANES:(h + 1) * LANES] for h in range(H_A)], axis=0).astype(BF16)
    full = _dot(qs, wq_ref[...])
    ql = jnp.concatenate(
        [full[h * t_dec:(h + 1) * t_dec, h * Q_LAT_W:(h + 1) * Q_LAT_W] for h in range(H_A)], axis=0)
    qlat = ql[:, :D_C].astype(BF16)
    qrope = ql[:, D_C:D_C + DR_A].astype(BF16)

    def partial_softmax(s, vals):
        m = jnp.max(s, -1, keepdims=True)
        p = jnp.exp(s - m)
        return m, jnp.sum(p, -1, keepdims=True), _dot(p.astype(BF16), vals)

    stats = []
    per_chunk = n_pages // n_chunks
    for c in range(n_chunks):
        pages = slice(c * per_chunk, (c + 1) * per_chunk)
        ck = jnp.concatenate([r[...].astype(BF16) for r in ckv_refs[pages]], axis=0)
        kt = jnp.concatenate([r[...].astype(BF16) for r in kpe_refs[pages]], axis=1)
        stats.append(partial_softmax(_dot_nt(qlat, ck) + _dot(qrope, kt), ck))

    pad = PAGE_SIZE - t_dec
    cn = jnp.concatenate([ckvn_ref[...], jnp.zeros((pad, D_C), F32)], axis=0).astype(BF16)
    kn = jnp.concatenate([kpen_ref[...][:, :DR_A], jnp.zeros((pad, DR_A), F32)], axis=0).astype(BF16)
    r = lax.broadcasted_iota(jnp.int32, (rows, PAGE_SIZE), 0) % t_dec
    cc = lax.broadcasted_iota(jnp.int32, (rows, PAGE_SIZE), 1)
    stats.append(partial_softmax(jnp.where(cc <= r, _dot_nt(qlat, cn) + _dot_nt(qrope, kn), NEG), cn))

    m = functools.reduce(jnp.maximum, [st[0] for st in stats])
    scales = [jnp.exp(st[0] - m) for st in stats]
    l = sum(sc * st[1] for sc, st in zip(scales, stats))
    acc = sum(sc * st[2] for sc, st in zip(scales, stats))
    o_lat = (acc / l).astype(BF16)
    full = _dot(o_lat, wuv_ref[...])
    rh = lax.broadcasted_iota(jnp.int32, full.shape, 0) // t_dec
    ch = lax.broadcasted_iota(jnp.int32, full.shape, 1) // DV_A
    full = jnp.where(rh == ch, full, 0.0)
    out = full[0:t_dec]
    for h in range(1, H_A):
        out = out + full[h * t_dec:(h + 1) * t_dec]
    o_ref[...] = out


def _mla_sample(page_table, q_s, ckv_new, kpe_new, wq2, wuv, cache_ckv, cache_kpe_t):
    nb, t_dec = q_s.shape[:2]
    n_pages = page_table.shape[1]

    def page_map(p):
        return lambda b, pt: (0, pt[b * n_pages + p], 0, 0)

    per_b = lambda w: pl.BlockSpec((None, t_dec, w), lambda b, pt: (b, 0, 0))
    grid_spec = pltpu.PrefetchScalarGridSpec(
        num_scalar_prefetch=1,
        grid=(nb,),
        in_specs=[per_b(H_A * LANES), per_b(D_C), per_b(LANES),
                  pl.BlockSpec(wq2.shape, lambda b, pt: (0, 0)),
                  pl.BlockSpec(wuv.shape, lambda b, pt: (0, 0))]
                 + [pl.BlockSpec((None, None, PAGE_SIZE, D_C), page_map(p)) for p in range(n_pages)]
                 + [pl.BlockSpec((None, None, DR_A, PAGE_SIZE), page_map(p)) for p in range(n_pages)],
        out_specs=per_b(H_A * DV_A),
    )
    return pl.pallas_call(
        functools.partial(_mla_sample_kernel, n_pages=n_pages, n_chunks=1, t_dec=t_dec),
        grid_spec=grid_spec,
        out_shape=jax.ShapeDtypeStruct((nb, t_dec, H_A * DV_A), F32),
        compiler_params=_params(1),
        name="mla_sample",
    )(page_table.reshape(-1), q_s, ckv_new, kpe_new, wq2, wuv,
      *([cache_ckv] * n_pages), *([cache_kpe_t] * n_pages))


def _proj_ln_kernel(x_ref, a_ref, b_ref, wa_ref, wb_ref, g_ref, beta_ref, o_ref):
    att = jnp.concatenate([a_ref[p] for p in range(a_ref.shape[0])], axis=1)
    y = _dot(att, wa_ref[...]) + _dot(b_ref[...], wb_ref[...])
    o_ref[...] = _layer_norm(ALPHA * x_ref[...] + y, g_ref[...], beta_ref[...])


def _proj_ln(x, a, b, wa, wb, g, beta, tm):
    m, d = x.shape
    row = lambda arr: pl.BlockSpec((tm, arr.shape[1]), lambda i: (i, 0))
    return pl.pallas_call(
        _proj_ln_kernel,
        grid=(m // tm,),
        in_specs=[row(x), pl.BlockSpec((a.shape[0], tm, LANES), lambda i: (0, i, 0)), row(b),
                  _full_spec(wa), _full_spec(wb), _full_spec(g), _full_spec(beta)],
        out_specs=pl.BlockSpec((tm, d), lambda i: (i, 0)),
        out_shape=jax.ShapeDtypeStruct((m, d), F32),
        compiler_params=_params(1),
        name="mix_out_ln",
    )(x, a, b, wa, wb, g, beta)


def _ffn_kernel(x_ref, wg_ref, wu_ref, wd_ref, g_ref, beta_ref, o_ref, ob_ref, *, n_chunks):
    x = x_ref[...]
    xb = x.astype(BF16)
    fc = wg_ref.shape[1] // n_chunks
    y = jnp.zeros(x.shape, F32)
    for c in range(n_chunks):
        cs = slice(c * fc, (c + 1) * fc)
        h = jax.nn.silu(_dot(xb, wg_ref[:, cs])) * _dot(xb, wu_ref[:, cs])
        y = y + _dot(h.astype(BF16), wd_ref[cs, :])
    out = _layer_norm(ALPHA * x + y, g_ref[...], beta_ref[...])
    o_ref[...] = out
    ob_ref[...] = out.astype(BF16)


def _ffn(x, wg, wu, wd, g, beta, tm, n_chunks):
    m, d = x.shape
    resident = lambda arr: pl.BlockSpec(arr.shape, lambda i: (0, 0), pipeline_mode=pl.Buffered(1))
    row = pl.BlockSpec((tm, d), lambda i: (i, 0))
    return pl.pallas_call(
        functools.partial(_ffn_kernel, n_chunks=n_chunks),
        grid=(m // tm,),
        in_specs=[row, resident(wg), resident(wu), resident(wd), resident(g), resident(beta)],
        out_specs=[row, row],
        out_shape=[jax.ShapeDtypeStruct((m, d), F32), jax.ShapeDtypeStruct((m, d), BF16)],
        compiler_params=_params(1),
        name="ffn_ln",
    )(x, wg, wu, wd, g, beta)


def _odd_in_prompt_kernel(x_ref, w_ref, q_ref, k_ref, v_ref, kv_ref, scr_ref, *, d, scale, kv_rows):
    tm = x_ref.shape[0]
    n = tm // d
    x = x_ref[...]
    for part, (o_ref, s) in enumerate(((q_ref, scale), (k_ref, 1.0), (v_ref, 1.0))):
        y = _dot(x, w_ref[:, part * W_C:(part + 1) * W_C])
        if part:
            kv_ref[(part - 1) * W_C:part * W_C, :] = y[tm - kv_rows:, :].T
        if d == 1:
            o_ref[0] = (y * s).astype(BF16)
        else:
            for c in range(W_C // LANES):
                scr_ref[c] = y[:, c * LANES:(c + 1) * LANES] * s
            for r in range(d):
                o_ref[r] = jnp.concatenate(
                    [scr_ref[c, pl.ds(r, n, stride=d), :] for c in range(W_C // LANES)], axis=1).astype(BF16)


def _odd_in_prompt(xb, w_g, g, n_batch, seq, tm):
    d = C_DILATIONS[g]
    win = min(C_WINDOWS[g], seq)
    tiles = seq // tm
    assert tm % d == 0 and (tm // d) % 16 == 0
    if win >= tm:
        assert win % tm == 0
        kv_rows, first = tm, (seq - win) // tm
    else:
        assert tm % win == 0
        kv_rows, first = win, tiles - 1
    dil = pl.BlockSpec((None, d, tm // d, W_C), lambda i: (i // tiles, 0, i % tiles, 0))
    kv_spec = pl.BlockSpec((None, 2 * W_C, kv_rows),
                           lambda i: (i // tiles, 0, jnp.maximum(i % tiles - first, 0)))
    dil_shape = jax.ShapeDtypeStruct((n_batch, d, seq // d, W_C), BF16)
    return pl.pallas_call(
        functools.partial(_odd_in_prompt_kernel, d=d, scale=DH_C ** -0.5, kv_rows=kv_rows),
        grid=(n_batch * tiles,),
        in_specs=[pl.BlockSpec((tm, xb.shape[1]), lambda i: (i, 0)), _full_spec(w_g)],
        out_specs=[dil, dil, dil, kv_spec],
        out_shape=[dil_shape, dil_shape, dil_shape, jax.ShapeDtypeStruct((n_batch, 2 * W_C, win), F32)],
        scratch_shapes=[pltpu.VMEM((W_C // LANES, tm, LANES), F32)],
        compiler_params=_params(1),
        name=f"odd_in_prompt_g{g}",
    )(xb, w_g)


def _odd_in_sample_kernel(x_ref, w_ref, q_ref, kv_ref, *, scale):
    y = _dot(x_ref[...], w_ref[...])
    q_ref[...] = y[:, :W_C] * scale
    kv_ref[...] = y[:, W_C:]


def _odd_in_sample(xb, w_g, g, row0, n_rows, tm):
    return pl.pallas_call(
        functools.partial(_odd_in_sample_kernel, scale=DH_C ** -0.5),
        grid=(n_rows // tm,),
        in_specs=[pl.BlockSpec((tm, xb.shape[1]), lambda i: (row0 // tm + i, 0)), _full_spec(w_g)],
        out_specs=[pl.BlockSpec((tm, W_C), lambda i: (i, 0)), pl.BlockSpec((tm, 2 * W_C), lambda i: (i, 0))],
        out_shape=[jax.ShapeDtypeStruct((n_rows, W_C), F32), jax.ShapeDtypeStruct((n_rows, 2 * W_C), F32)],
        compiler_params=_params(1),
        name=f"odd_in_sample_g{g}",
    )(xb, w_g)


def _dsw_prompt_kernel(q_ref, k_ref, v_ref, bias_ref, o_ref, lse_ref, *, n_blk):
    blk = C_BLOCK
    lane = lax.broadcasted_iota(jnp.int32, (blk, LANES), 1)
    lo = lane < DH_C

    def block(r, n, first):
        if first:
            qs = ks = pl.ds(0, blk)
        else:
            qs = pl.ds(pl.multiple_of(n * blk, blk), blk)
            ks = pl.ds(pl.multiple_of((n - 1) * blk, blk), 2 * blk)
        parts = []
        for hp in range(H_C // 2):
            hs = slice(hp * LANES, (hp + 1) * LANES)
            qp = q_ref[r, qs, hs]
            zero = jnp.zeros_like(qp)
            qq = jnp.concatenate([jnp.where(lo, qp, zero), jnp.where(lo, zero, qp)], axis=0)
            parts.append(_dot_nt(qq, k_ref[r, ks, hs]))
        s = jnp.concatenate(parts, axis=0) + (bias_ref[:, blk:] if first else bias_ref[...])
        m = jnp.max(s, -1, keepdims=True)
        p = jnp.exp(s - m)
        l = jnp.sum(p, -1, keepdims=True)
        pn = (p * (1.0 / l)).astype(BF16)
        lse = m + jnp.log(l)
        lse_tile = jnp.zeros((blk, LANES), F32)
        for hp in range(H_C // 2):
            hs = slice(hp * LANES, (hp + 1) * LANES)
            o2 = _dot(pn[2 * hp * blk:(2 * hp + 2) * blk], v_ref[r, ks, hs])
            o_ref[r, qs, hs] = jnp.where(lo, o2[:blk], o2[blk:]).astype(BF16)
            for h in (2 * hp, 2 * hp + 1):
                lse_tile = jnp.where(lane == h, lse[h * blk:(h + 1) * blk], lse_tile)
        lse_ref[r, qs, :] = lse_tile

    def residue(r, carry):
        block(r, 0, True)
        if n_blk > 1:
            def body(n, c):
                block(r, n, False)
                return c
            lax.fori_loop(1, n_blk, body, 0)
        return carry

    lax.fori_loop(0, q_ref.shape[0], residue, 0)


def _dsw_prompt(q, k, v, bias, g):
    n_batch, d, length, w = q.shape
    assert length % C_BLOCK == 0
    spec = lambda width: pl.BlockSpec((None, d, length, width), lambda b: (b, 0, 0, 0))
    return pl.pallas_call(
        functools.partial(_dsw_prompt_kernel, n_blk=length // C_BLOCK),
        grid=(n_batch,),
        in_specs=[spec(w), spec(w), spec(w), _full_spec(bias)],
        out_specs=[spec(w), spec(LANES)],
        out_shape=[jax.ShapeDtypeStruct((n_batch, d, length, w), BF16),
                   jax.ShapeDtypeStruct((n_batch, d, length, LANES), F32)],
        compiler_params=_params(1),
        name=f"dsw_prompt_g{g}",
    )(q, k, v, bias)


def _dsw_sample_kernel(q_ref, qh_ref, kvn_ref, buf_ref, mask_ref, maskn_ref, o_ref, lse_ref, *, t_dec):
    rows = H_C * t_dec
    rh = lax.broadcasted_iota(jnp.int32, (rows, W_C), 0) // t_dec
    ch = lax.broadcasted_iota(jnp.int32, (rows, W_C), 1) // DH_C
    same = rh == ch
    lane = lax.broadcasted_iota(jnp.int32, (rows, LANES), 1)
    rl = lax.broadcasted_iota(jnp.int32, (rows, LANES), 0) // t_dec
    pad = LANES - t_dec

    q = q_ref[...]
    qbd = jnp.where(same, jnp.concatenate([q] * H_C, axis=0), 0.0).astype(BF16)
    kvn = kvn_ref[...]
    kn = jnp.concatenate([kvn[:, :W_C], jnp.zeros((pad, W_C), F32)], axis=0).astype(BF16)
    vn = jnp.concatenate([kvn[:, W_C:], jnp.zeros((pad, W_C), F32)], axis=0).astype(BF16)
    s_n = _dot_nt(qbd, kn) + maskn_ref[...]

    s_b = jnp.concatenate(
        [_dot(qh_ref[h].astype(BF16), buf_ref[0, h].astype(BF16)) for h in range(H_C)], axis=0) + mask_ref[...]

    m = jnp.maximum(jnp.max(s_b, -1, keepdims=True), jnp.max(s_n, -1, keepdims=True))
    p_b = jnp.exp(s_b - m)
    p_n = jnp.exp(s_n - m)
    l = jnp.sum(p_b, -1, keepdims=True) + jnp.sum(p_n, -1, keepdims=True)
    p_b = p_b / l
    o_n = jnp.where(same, _dot((p_n / l).astype(BF16), vn), 0.0)
    lse = jnp.where(lane == rl, m + jnp.log(l), 0.0)
    o_acc = o_n[0:t_dec]
    lse_acc = lse[0:t_dec]
    for h in range(1, H_C):
        o_acc = o_acc + o_n[h * t_dec:(h + 1) * t_dec]
        lse_acc = lse_acc + lse[h * t_dec:(h + 1) * t_dec]
    o_b = jnp.concatenate(
        [_dot_nt(p_b[h * t_dec:(h + 1) * t_dec].astype(BF16), buf_ref[1, h].astype(BF16)) for h in range(H_C)],
        axis=1)
    o_ref[...] = o_acc + o_b
    lse_ref[...] = lse_acc


def _dsw_sample(q_s, q_hm, kv_new, buf_t, mask, mask_new):
    nb, _, t_dec, _ = q_hm.shape
    ms = nb * t_dec
    row = lambda width: pl.BlockSpec((t_dec, width), lambda b: (b, 0))
    return pl.pallas_call(
        functools.partial(_dsw_sample_kernel, t_dec=t_dec),
        grid=(nb,),
        in_specs=[row(W_C), pl.BlockSpec((None,) + q_hm.shape[1:], lambda b: (b, 0, 0, 0)), row(2 * W_C),
                  pl.BlockSpec((None,) + buf_t.shape[1:], lambda b: (b, 0, 0, 0, 0)),
                  _full_spec(mask), _full_spec(mask_new)],
        out_specs=[row(W_C), row(LANES)],
        out_shape=[jax.ShapeDtypeStruct((ms, W_C), F32), jax.ShapeDtypeStruct((ms, LANES), F32)],
        compiler_params=_params(1),
        name="dsw_sample",
    )(q_s, q_hm, kv_new, buf_t, mask, mask_new)


def _split_bf16(x):
    hi = x.astype(BF16)
    return hi, (x - hi.astype(F32)).astype(BF16)


def _interleave(ref, scr_ref):
    d, n, width = ref.shape
    if d == 1:
        return ref[0].astype(F32)
    slabs = width // LANES
    for r in range(d):
        val = ref[r].astype(F32)
        for c in range(slabs):
            scr_ref[c, pl.ds(r, n, stride=d), :] = val[:, c * LANES:(c + 1) * LANES]
    return jnp.concatenate([scr_ref[c] for c in range(slabs)], axis=1)


def _merge_kernel(x_ref, o0_ref, o1_ref, o2_ref, l0_ref, l1_ref, l2_ref, e_ref, w_ref, g_ref, beta_ref,
                  wr_hi_ref, wr_lo_ref, *rest, n_alias):
    y_ref, yb_ref, route_ref, o_scr, l_scr = rest[n_alias:]
    lses = [_interleave(l_ref, l_scr) for l_ref in (l0_ref, l1_ref, l2_ref)]
    mx = jnp.maximum(jnp.maximum(lses[0], lses[1]), lses[2])
    es = [jnp.exp(l - mx) for l in lses]
    den = es[0] + es[1] + es[2]
    acc = None
    for e, o_ref in zip(es, (o0_ref, o1_ref, o2_ref)):
        hi, lo = _split_bf16(e / den)
        wexp = _dot(hi, e_ref[...]) + _dot(lo, e_ref[...])
        term = wexp * _interleave(o_ref, o_scr)
        acc = term if acc is None else acc + term
    y = _dot(acc.astype(BF16), w_ref[...])
    x2 = _layer_norm(ALPHA * x_ref[...] + y, g_ref[...], beta_ref[...])
    y_ref[...] = x2
    yb_ref[...] = x2.astype(BF16)

    xh, xl = _split_bf16(x2)
    logits = _dot(xh, wr_hi_ref[...]) + _dot(xl, wr_hi_ref[...]) + _dot(xh, wr_lo_ref[...])
    lane = lax.broadcasted_iota(jnp.int32, logits.shape, 1)
    logits = jnp.where(lane < N_EXPERTS, logits, -jnp.inf)
    v1 = jnp.max(logits, -1, keepdims=True)
    i1 = jnp.min(jnp.where(logits == v1, lane, LANES), -1, keepdims=True)
    rest_l = jnp.where(lane == i1, -jnp.inf, logits)
    v2 = jnp.max(rest_l, -1, keepdims=True)
    i2 = jnp.min(jnp.where(rest_l == v2, lane, LANES), -1, keepdims=True)
    e2 = jnp.exp(v2 - v1)
    g1 = 1.0 / (1.0 + e2)
    g2 = e2 / (1.0 + e2)
    route = jnp.where(lane == 0, i1.astype(F32), 0.0)
    route = jnp.where(lane == 1, i2.astype(F32), route)
    route = jnp.where(lane == 2, g1, route)
    route = jnp.where(lane == 3, g2, route)
    route_ref[...] = route


def _merge(x, os_, lses, weights, tm, row0, n_rows, tiles_per_batch, prev):
    m, d_model = x.shape
    base = row0 // tm

    def dil(arr):
        _, d, _, width = arr.shape
        return pl.BlockSpec((None, d, tm // d, width),
                            lambda i: (i // tiles_per_batch, 0, i % tiles_per_batch, 0))

    row = lambda width: pl.BlockSpec((tm, width), lambda i: (base + i, 0))
    outs = [jax.ShapeDtypeStruct((m, d_model), F32), jax.ShapeDtypeStruct((m, d_model), BF16),
            jax.ShapeDtypeStruct((m, LANES), F32)]
    n_alias = 0 if prev is None else len(prev)
    n_in = 7 + len(weights)
    return pl.pallas_call(
        functools.partial(_merge_kernel, n_alias=n_alias),
        grid=(n_rows // tm,),
        in_specs=[row(d_model)] + [dil(o) for o in os_] + [dil(l) for l in lses]
                 + [_full_spec(a) for a in weights] + [pl.BlockSpec(memory_space=pl.ANY)] * n_alias,
        out_specs=[row(d_model), row(d_model), row(LANES)],
        out_shape=outs,
        scratch_shapes=[pltpu.VMEM((W_C // LANES, tm, LANES), F32), pltpu.VMEM((1, tm, LANES), F32)],
        input_output_aliases={n_in + j: j for j in range(n_alias)},
        compiler_params=_params(1),
        name="merge_out_ln_route",
    )(x, *os_, *lses, *weights, *(prev or ()))


def _moe_kernel(te_ref, nt_ref, x_ref, wg_ref, wu_ref, wd_ref, o_ref, *, n_chunks):
    t = pl.program_id(0)

    @pl.when(t < nt_ref[0])
    def _():
        xb = x_ref[...]
        fc = wg_ref.shape[1] // n_chunks
        y = jnp.zeros(o_ref.shape, F32)
        for c in range(n_chunks):
            cs = slice(c * fc, (c + 1) * fc)
            h = jax.nn.silu(_dot(xb, wg_ref[:, cs])) * _dot(xb, wu_ref[:, cs])
            y = y + _dot(h.astype(BF16), wd_ref[cs, :])
        o_ref[...] = y

    @pl.when(t >= nt_ref[0])
    def _():
        o_ref[...] = jnp.zeros(o_ref.shape, F32)


def _moe(tile_expert, n_tiles_used, xs, wg, wu, wd, tm, n_chunks):
    p, d = xs.shape
    f = wg.shape[2]
    grid_spec = pltpu.PrefetchScalarGridSpec(
        num_scalar_prefetch=2,
        grid=(p // tm,),
        in_specs=[pl.BlockSpec((tm, d), lambda t, te, nt: (t, 0)),
                  pl.BlockSpec((None, d, f), lambda t, te, nt: (te[t], 0, 0), pipeline_mode=pl.Buffered(1)),
                  pl.BlockSpec((None, d, f), lambda t, te, nt: (te[t], 0, 0), pipeline_mode=pl.Buffered(1)),
                  pl.BlockSpec((None, f, d), lambda t, te, nt: (te[t], 0, 0), pipeline_mode=pl.Buffered(1))],
        out_specs=pl.BlockSpec((tm, d), lambda t, te, nt: (t, 0)),
    )
    return pl.pallas_call(
        functools.partial(_moe_kernel, n_chunks=n_chunks),
        grid_spec=grid_spec,
        out_shape=jax.ShapeDtypeStruct((p, d), F32),
        compiler_params=_params(1),
        name="moe_experts",
    )(tile_expert, n_tiles_used, xs, wg, wu, wd)


def _final_kernel(x_ref, ya_ref, yb_ref, route_ref, g_ref, beta_ref, op_ref, os_ref, *, n_prompt_tiles):
    route = route_ref[...]
    y = route[:, 2:3] * ya_ref[...] + route[:, 3:4] * yb_ref[...]
    out = _layer_norm(ALPHA * x_ref[...] + y, g_ref[...], beta_ref[...])
    i = pl.program_id(0)

    @pl.when(i < n_prompt_tiles)
    def _():
        op_ref[...] = out

    @pl.when(i >= n_prompt_tiles)
    def _():
        os_ref[...] = out


def _final(x, ya, yb, route, g, beta, tm, n_prompt_rows):
    m, d = x.shape
    n_p = n_prompt_rows // tm
    row = lambda arr: pl.BlockSpec((tm, arr.shape[1]), lambda i: (i, 0))
    return pl.pallas_call(
        functools.partial(_final_kernel, n_prompt_tiles=n_p),
        grid=(m // tm,),
        in_specs=[row(x), row(ya), row(yb), row(route), _full_spec(g), _full_spec(beta)],
        out_specs=[pl.BlockSpec((tm, d), lambda i: (jnp.minimum(i, n_p - 1), 0)),
                   pl.BlockSpec((tm, d), lambda i: (jnp.maximum(i - n_p, 0), 0))],
        out_shape=[jax.ShapeDtypeStruct((n_prompt_rows, d), F32),
                   jax.ShapeDtypeStruct((m - n_prompt_rows, d), F32)],
        compiler_params=_params(1),
        name="moe_combine_ln",
    )(x, ya, yb, route, g, beta)


def _rope_table(pos, scale):
    half = DR_A // 2
    inv = ROPE_BASE ** (-jnp.arange(half, dtype=F32) / half)
    ang = pos.astype(F32)[:, None] * inv[None, :]
    cos, sin = jnp.cos(ang), jnp.sin(ang)
    n = pos.shape[0]
    z = lambda w: jnp.zeros((n, w), F32)
    one = jnp.ones((n, DN_A), F32)
    cq = jnp.concatenate([one, cos, cos, z(32)], 1) * scale
    s1q = jnp.concatenate([z(DN_A), -sin, z(16), z(32)], 1) * scale
    s2q = jnp.concatenate([z(DN_A), z(16), sin, z(32)], 1) * scale
    ck = jnp.concatenate([cos, cos, z(96)], 1)
    s1k = jnp.concatenate([-sin, z(112)], 1)
    s2k = jnp.concatenate([z(16), sin, z(96)], 1)
    return jnp.concatenate([cq, s1q, s2q, ck, s1k, s2k], 1)


def _t5_bucket(dist):
    max_exact = N_BUCKETS // 2
    log_ratio = jnp.log(jnp.maximum(dist, 1).astype(F32) / max_exact) / math.log(MAX_DISTANCE / max_exact)
    large = jnp.minimum(max_exact + (log_ratio * (N_BUCKETS - max_exact)).astype(jnp.int32), N_BUCKETS - 1)
    return jnp.where(dist < max_exact, dist, large)


def _group_bias(rel_bias, g):
    d = C_DILATIONS[g]
    dist = d * jnp.arange(C_WINDOWS[g] // d + 1, dtype=jnp.int32)
    onehot = (_t5_bucket(dist)[:, None] == jnp.arange(N_BUCKETS)[None, :]).astype(F32)
    table = rel_bias[:, g * H_C:(g + 1) * H_C]
    return jnp.transpose(jnp.dot(onehot, table, precision=lax.Precision.HIGHEST))


def _prompt_bias(bias):
    n_back = bias.shape[1] - 1
    a = np.arange(C_BLOCK)[:, None]
    c = np.arange(2 * C_BLOCK)[None, :]
    j = a + C_BLOCK - c
    band = (j >= 0) & (j <= n_back)
    full = _select_columns(bias, np.clip(j, 0, n_back).reshape(-1))
    return jnp.where(band[None], full.reshape(-1, C_BLOCK, 2 * C_BLOCK), NEG).reshape(-1, 2 * C_BLOCK)


def _select_columns(bias, cols):
    sel = (jnp.arange(bias.shape[1], dtype=jnp.int32)[:, None] == jnp.asarray(cols, jnp.int32)[None, :])
    return jnp.dot(bias.astype(F32), sel.astype(F32), precision=lax.Precision.HIGHEST)


def _sample_masks(bias, g, n_buf, t_dec):
    d = C_DILATIONS[g]
    n_back = bias.shape[1] - 1
    t = np.arange(t_dec)[:, None]

    def build(key_pos):
        diff = n_buf + t - key_pos[None, :]
        ok = (diff >= 0) & (diff % d == 0) & (diff // d <= n_back)
        full = _select_columns(bias, np.clip(diff // d, 0, n_back).reshape(-1))
        mt = jnp.where(ok[None], full.reshape(H_C, t_dec, -1), NEG)
        return mt.reshape(H_C * t_dec, -1)

    mask_new = build(n_buf + np.arange(LANES))
    col_ok = (np.arange(LANES) < t_dec)[None, :]
    return build(np.arange(n_buf)), jnp.where(col_ok, mask_new, NEG)


def _prep_even(e_w_in, e_q_norm_g, e_kv_norm_g, e_w_uq, e_w_uk, e_w_uv, e_v_ln_g, e_v_ln_b, e_w_s, e_b_s, t_dec):
    o_kr = D_CQ + D_C
    o_uv = o_kr + DR_A
    d_model = e_w_in.shape[0]
    w_in = jnp.concatenate([e_w_in[:, :o_kr], e_w_in[:, o_uv:], e_w_in[:, o_kr:o_uv],
                            jnp.zeros((d_model, LANES - DR_A), F32)], axis=1).astype(BF16)
    w_uq = jnp.pad(e_w_uq, ((0, 0), (0, 0), (0, LANES - DN_A - DR_A))).reshape(D_CQ, H_A * LANES).astype(BF16)
    w_uk_pad = jnp.pad(e_w_uk, ((0, 0), (0, 0), (0, LANES - DN_A))).reshape(D_C, H_A * LANES)
    w_uv = e_w_uv.reshape(D_C, H_A * DV_A)
    place = np.zeros((LANES, H_A, LANES), np.float32)
    for i in range(DR_A):
        place[i, :, DN_A + i] = 1.0
    place = place.reshape(LANES, H_A * LANES)
    w_kv = jnp.concatenate([
        jnp.concatenate([w_uk_pad, w_uv], axis=1),
        jnp.concatenate([jnp.asarray(place), jnp.zeros((LANES, H_A * DV_A), F32)], axis=1)], axis=0).astype(BF16)

    tril = np.tril(np.ones((CHUNK, CHUNK), bool))
    wg_p = jnp.where(tril[None], e_w_s, 0.0)
    small = jnp.where(np.tril(np.ones((t_dec, t_dec), bool))[None], e_w_s[:, :t_dec, :t_dec], 0.0)
    eye = jnp.eye(CHUNK // t_dec, dtype=F32)
    wg_s = jnp.einsum("ab,gts->gatbs", eye, small).reshape(G_B, CHUNK, CHUNK)
    wgate = jnp.stack([wg_p, wg_s]).astype(BF16)
    bg_p = jnp.broadcast_to(e_b_s[:, :, None], (G_B, CHUNK, LANES))
    bg_s = jnp.broadcast_to(jnp.tile(e_b_s[:, :t_dec], (1, CHUNK // t_dec))[:, :, None], (G_B, CHUNK, LANES))
    bgate = jnp.stack([bg_p, bg_s]).astype(F32)

    sel = np.zeros((LANES, Q_LAT_W), np.float32)
    for i in range(DR_A):
        sel[DN_A + i, D_C + i] = 1.0
    wq2 = jnp.broadcast_to(jnp.asarray(sel)[:, None, :], (LANES, H_A, Q_LAT_W))
    uk_t = jnp.transpose(e_w_uk, (2, 1, 0))
    wq2 = wq2.at[:DN_A, :, :D_C].add(uk_t)
    wq2 = wq2.reshape(LANES, H_A * Q_LAT_W).astype(BF16)
    return dict(w_in=w_in, gq=e_q_norm_g.reshape(1, -1), gkv=e_kv_norm_g.reshape(1, -1), w_uq=w_uq, w_kv=w_kv,
                vg=e_v_ln_g.reshape(1, -1), vb=e_v_ln_b.reshape(1, -1), wgate=wgate, bgate=bgate,
                wq2=wq2, wuv=w_uv.astype(BF16))


def _pick_tile(m_prompt, m_sample, cap):
    t = cap
    while m_prompt % t or m_sample % t:
        t //= 2
    assert t >= CHUNK
    return t


def kernel(x_prompt, x_sample, cache_mla_ckv, cache_mla_kpe, cache_dsw_kv0, cache_dsw_kv1, cache_dsw_kv2, page_table, ln_g, ln_b, e_w_in, e_q_norm_g, e_kv_norm_g, e_w_uq, e_w_uk, e_w_uv, e_v_ln_g, e_v_ln_b, e_w_s, e_b_s, e_w_out, ffn_w_gate, ffn_w_up, ffn_w_down, o_w_in, o_w_out, rel_bias, moe_w_router, moe_w_gate, moe_w_up, moe_w_down):
    nb_p, seq, d_model = x_prompt.shape
    nb_s, t_dec, _ = x_sample.shape
    past_len = page_table.shape[1] * PAGE_SIZE
    mp, ms = nb_p * seq, nb_s * t_dec
    m = mp + ms
    tm = _pick_tile(seq, ms, 512)
    tm_p = _pick_tile(seq, seq, 512)
    tm_s = _pick_tile(ms, ms, 512)
    assert CHUNK % t_dec == 0 and seq % CHUNK == 0 and mp % tm_s == 0
    ffn_chunks = 4

    x0 = jnp.concatenate([x_prompt.reshape(mp, d_model), x_sample.reshape(ms, d_model)], axis=0)
    ln = lambda l, j: (ln_g[l, j].reshape(1, -1), ln_b[l, j].reshape(1, -1))

    prep = _prep_even(e_w_in[0], e_q_norm_g[0], e_kv_norm_g[0], e_w_uq[0], e_w_uk[0], e_w_uv[0], e_v_ln_g[0],
                      e_v_ln_b[0], e_w_s[0], e_b_s[0], t_dec)
    scale = (DN_A + DR_A) ** -0.5
    pos = jnp.concatenate([jnp.arange(seq, dtype=jnp.int32),
                           past_len + (jnp.arange(tm, dtype=jnp.int32) % t_dec)])
    tab = _rope_table(pos, scale)
    q, k, v, ckv, kpe, gated, vln = _even_in(x0, prep, tab, tm, mp // tm, seq // tm)

    att = _mla_prompt(q, k, v, nb_p, seq, min(256, seq))
    q_s = jnp.transpose(q[:, mp:].astype(F32), (1, 0, 2)).reshape(nb_s, t_dec, H_A * LANES)
    ckv_s = ckv[mp:].reshape(nb_s, t_dec, D_C)
    kpe_s = kpe[mp:].reshape(nb_s, t_dec, LANES)
    att_s = _mla_sample(page_table, q_s, ckv_s, kpe_s, prep["wq2"], prep["wuv"], cache_mla_ckv,
                        jnp.swapaxes(cache_mla_kpe, 2, 3))
    att_s = jnp.transpose(att_s.reshape(ms, H_A // 2, LANES), (1, 0, 2)).astype(BF16)
    att = lax.dynamic_update_slice(att, att_s, (0, mp, 0))

    w_out_e = e_w_out[0].astype(BF16)
    n_att = H_A * DV_A
    x1 = _proj_ln(x0, att, gated, w_out_e[:n_att], w_out_e[n_att:], *ln(0, 0), tm)
    x1, x1b = _ffn(x1, ffn_w_gate[0].astype(BF16), ffn_w_up[0].astype(BF16), ffn_w_down[0].astype(BF16),
                   *ln(0, 1), tm, ffn_chunks)

    ng = N_GROUPS_C
    w_in_o = o_w_in[0].reshape(d_model, ng, 3 * W_C).astype(BF16)
    caches = (cache_dsw_kv0, cache_dsw_kv1, cache_dsw_kv2)
    os_p, lses_p, os_s, lses_s, dsw_p, dsw_s = [], [], [], [], [], []
    for g in range(ng):
        w_g = w_in_o[:, g]
        bias = _group_bias(rel_bias, g)
        qd, kd, vd, kv_tail = _odd_in_prompt(x1b, w_g, g, nb_p, seq, tm_p)
        win = min(C_WINDOWS[g], seq)
        dsw_p.append(jnp.transpose(kv_tail.reshape(nb_p, 2, H_C, DH_C, win), (0, 4, 1, 2, 3))[None])
        o_g, lse_g = _dsw_prompt(qd, kd, vd, _prompt_bias(bias), g)
        os_p.append(o_g)
        lses_p.append(lse_g)

        q_new, kv_new = _odd_in_sample(x1b, w_g, g, mp, ms, tm_s)
        dsw_s.append(kv_new.reshape(1, nb_s, t_dec, 2, H_C, DH_C))
        n_buf = caches[g].shape[2]
        buf_t = jnp.transpose(caches[g][0], (0, 2, 3, 4, 1))
        q_hm = jnp.transpose(q_new.reshape(nb_s, t_dec, H_C, DH_C), (0, 2, 1, 3))
        mask, mask_new = _sample_masks(bias, g, n_buf, t_dec)
        o_g, lse_g = _dsw_sample(q_new, q_hm, kv_new, buf_t, mask, mask_new)
        os_s.append(o_g.reshape(1, 1, ms, W_C))
        lses_s.append(lse_g.reshape(1, 1, ms, LANES))

    expand = np.zeros((LANES, W_C), np.float32)
    for h in range(H_C):
        expand[h, h * DH_C:(h + 1) * DH_C] = 1.0
    wr = jnp.pad(moe_w_router[0], ((0, 0), (0, LANES - N_EXPERTS)))
    wr_hi, wr_lo = _split_bf16(wr)
    weights = [jnp.asarray(expand, BF16), o_w_out[0].astype(BF16), *ln(1, 0), wr_hi, wr_lo]
    merged = _merge(x1, os_p, lses_p, weights, tm_p, 0, mp, seq // tm_p, None)
    x2, x2b, route = _merge(x1, os_s, lses_s, weights, tm_s, mp, ms, ms // tm_s, merged)

    tmoe = 512
    idx = route[:, :TOP_K].astype(jnp.int32)
    flat_e = idx.reshape(-1)
    onehot = (flat_e[:, None] == jnp.arange(N_EXPERTS)[None, :]).astype(jnp.int32)
    csum = jnp.cumsum(onehot, axis=0)
    counts = csum[-1]
    rank = jnp.sum(csum * onehot, axis=1) - 1
    padded = ((counts + tmoe - 1) // tmoe) * tmoe
    ends = jnp.cumsum(padded)
    starts = ends - padded
    pos_sorted = jnp.sum(starts[None, :] * onehot, axis=1) + rank
    p_rows = ((TOP_K * m + tmoe - 1) // tmoe + N_EXPERTS) * tmoe
    n_tiles = p_rows // tmoe
    tile_start = jnp.arange(n_tiles, dtype=jnp.int32) * tmoe
    tile_expert = jnp.minimum(jnp.sum(tile_start[:, None] >= ends[None, :], axis=1), N_EXPERTS - 1).astype(jnp.int32)
    n_used = (ends[-1] // tmoe).astype(jnp.int32).reshape(1)
    order = jnp.argsort(flat_e, stable=True).astype(jnp.int32)
    shift = starts - (jnp.cumsum(counts) - counts)
    row_shift = jnp.repeat(jnp.sum(shift[None, :] * (tile_expert[:, None] == jnp.arange(N_EXPERTS)[None, :]), axis=1),
                           tmoe)
    q_idx = jnp.clip(jnp.arange(p_rows, dtype=jnp.int32) - row_shift, 0, TOP_K * m - 1)
    src = order.at[q_idx].get(mode="promise_in_bounds") // TOP_K
    xs = x2b.at[src].get(mode="promise_in_bounds")
    ys = _moe(tile_expert, n_used, xs, moe_w_gate[0].astype(BF16), moe_w_up[0].astype(BF16),
              moe_w_down[0].astype(BF16), tmoe, ffn_chunks)
    pos2 = pos_sorted.reshape(m, TOP_K)
    ya = ys.at[pos2[:, 0]].get(mode="promise_in_bounds")
    yb = ys.at[pos2[:, 1]].get(mode="promise_in_bounds")
    y_p, y_s = _final(x2, ya, yb, route, *ln(1, 1), tm, mp)

    y_prompt = y_p.reshape(nb_p, seq, d_model)
    y_sample = y_s.reshape(nb_s, t_dec, d_model)
    n_pg = seq // PAGE_SIZE
    new_ckv_prompt = ckv[:mp].reshape(1, nb_p, n_pg, PAGE_SIZE, D_C)
    new_kpe_prompt = kpe[:mp, :DR_A].reshape(1, nb_p, n_pg, PAGE_SIZE, DR_A)
    new_ckv_sample = ckv_s.reshape(1, nb_s, t_dec, D_C)
    new_kpe_sample = kpe_s[:, :, :DR_A].reshape(1, nb_s, t_dec, DR_A)
    new_gate_v_sample = vln[mp:].reshape(1, nb_s, t_dec, D_B)
    return (y_prompt, y_sample, new_ckv_prompt, new_kpe_prompt, new_ckv_sample, new_kpe_sample, new_gate_v_sample,
            dsw_p[0], dsw_p[1], dsw_p[2], dsw_s[0], dsw_s[1], dsw_s[2])
```

```python
import functools
import math

import numpy as np
import jax
import jax.numpy as jnp
from jax import lax
from jax.experimental import pallas as pl
from jax.experimental.pallas import tpu as pltpu

BF16 = jnp.bfloat16
F32 = jnp.float32

H_A, DN_A, DR_A, DV_A = 8, 64, 32, 64
D_CQ, D_C = 384, 256
ROPE_BASE = 10000.0
CHUNK, G_B, D_B = 128, 4, 512
C_WINDOWS = (128, 512, 2048)
C_DILATIONS = (1, 4, 16)
N_GROUPS_C = 3
H_C, DH_C = 16, 64
W_C = H_C * DH_C
C_BLOCK = 128
N_BUCKETS, MAX_DISTANCE = 32, 2048
N_EXPERTS, TOP_K = 8, 2
DEPTH = 2
ALPHA = (2 * DEPTH) ** 0.25
LN_EPS = 1e-5
RMS_EPS = 1e-6
PAGE_SIZE = 128
NEG = -1e30

LANES = 128
MXU_DIM = 256
V7X_VMEM_BYTES = 64 * 2**20
VMEM_LIMIT = 56 * 2**20


def _dot(a, b):
    return jnp.dot(a, b, preferred_element_type=F32)


def _dot_nt(a, b):
    return lax.dot_general(a, b, (((1,), (1,)), ((), ())), preferred_element_type=F32)


def _params(n_axes, vmem=VMEM_LIMIT):
    return pltpu.CompilerParams(dimension_semantics=("arbitrary",) * n_axes, vmem_limit_bytes=vmem)


def _layer_norm(x, g, b):
    mu = jnp.mean(x, -1, keepdims=True)
    xc = x - mu
    var = jnp.mean(xc * xc, -1, keepdims=True)
    return xc * lax.rsqrt(var + LN_EPS) * g + b


def _rms_norm(x, g):
    return x * lax.rsqrt(jnp.mean(x * x, -1, keepdims=True) + RMS_EPS) * g


def _full_spec(a):
    return pl.BlockSpec(a.shape, lambda *_: (0,) * a.ndim)


def _rope_apply(x, c, s1, s2):
    return x * c + pltpu.roll(x, LANES - 16, 1) * s1 + pltpu.roll(x, 16, 1) * s2


def _even_in_kernel(xp_ref, xs_ref, w_in_ref, gq_ref, gkv_ref, w_uq_ref, w_kv_ref, vg_ref, vb_ref, tab_ref,
                    wgate_ref, bgate_ref,
                    q_ref, k_ref, v_ref, ckv_ref, kpe_ref, gated_ref, vln_ref, *, n_prompt_tiles):
    tm = xp_ref.shape[0]
    x = jnp.where(pl.program_id(0) < n_prompt_tiles, xp_ref[...], xs_ref[...])
    h = _dot(x.astype(BF16), w_in_ref[...])
    c_q = h[:, :D_CQ]
    c_kv = h[:, D_CQ:D_CQ + D_C]
    o_u = D_CQ + D_C
    u = jax.nn.gelu(h[:, o_u:o_u + D_B])
    vv = jax.nn.gelu(h[:, o_u + D_B:o_u + 2 * D_B])
    kr = h[:, o_u + 2 * D_B:]

    tab = tab_ref[...]
    cq, s1q, s2q = tab[:, 0:128], tab[:, 128:256], tab[:, 256:384]
    ck, s1k, s2k = tab[:, 384:512], tab[:, 512:640], tab[:, 640:768]

    q = _dot(_rms_norm(c_q, gq_ref[...]).astype(BF16), w_uq_ref[...])
    for hb in range(H_A):
        q_ref[hb] = _rope_apply(q[:, hb * LANES:(hb + 1) * LANES], cq, s1q, s2q).astype(BF16)

    ckv = _rms_norm(c_kv, gkv_ref[...])
    ckv_ref[...] = ckv
    kpe = _rope_apply(kr, ck, s1k, s2k)
    kpe_ref[...] = kpe
    kv = _dot(jnp.concatenate([ckv.astype(BF16), kpe.astype(BF16)], axis=1), w_kv_ref[...])
    for hb in range(H_A):
        k_ref[hb] = kv[:, hb * LANES:(hb + 1) * LANES].astype(BF16)
    for p in range(H_A // 2):
        v_ref[p] = kv[:, (H_A + p) * LANES:(H_A + p + 1) * LANES].astype(BF16)

    vln = _layer_norm(vv, vg_ref[...], vb_ref[...])
    vln_ref[...] = vln
    vb16 = vln.astype(BF16)
    gw = D_B // G_B
    for c in range(tm // CHUNK):
        rs = slice(c * CHUNK, (c + 1) * CHUNK)
        for g in range(G_B):
            cs = slice(g * gw, (g + 1) * gw)
            f = _dot(wgate_ref[g], vb16[rs, cs]) + bgate_ref[g]
            gated_ref[rs, cs] = (u[rs, cs] * f).astype(BF16)


def _even_in(xp, xs, prep, tab, tm, tab_blocks):
    (mp, d_model), ms = xp.shape, xs.shape[0]
    m = mp + ms
    n_prompt_tiles = mp // tm

    def tab_map(i):
        return (jnp.where(i < n_prompt_tiles, i % tab_blocks, tab_blocks), 0)

    def gate_map(i):
        return (jnp.where(i < n_prompt_tiles, 0, 1), 0, 0, 0)

    row = lambda w: pl.BlockSpec((tm, w), lambda i: (i, 0))
    heads = lambda n: pl.BlockSpec((n, tm, LANES), lambda i: (0, i, 0))
    outs = [
        jax.ShapeDtypeStruct((H_A, m, LANES), BF16),
        jax.ShapeDtypeStruct((H_A, m, LANES), BF16),
        jax.ShapeDtypeStruct((H_A // 2, m, LANES), BF16),
        jax.ShapeDtypeStruct((m, D_C), F32),
        jax.ShapeDtypeStruct((m, LANES), F32),
        jax.ShapeDtypeStruct((m, D_B), BF16),
        jax.ShapeDtypeStruct((m, D_B), F32),
    ]
    weights = [prep[n] for n in ("w_in", "gq", "gkv", "w_uq", "w_kv", "vg", "vb")]
    return pl.pallas_call(
        functools.partial(_even_in_kernel, n_prompt_tiles=n_prompt_tiles),
        grid=(m // tm,),
        in_specs=_two_group_specs(tm, d_model, n_prompt_tiles) + [_full_spec(a) for a in weights]
                 + [pl.BlockSpec((tm, 6 * LANES), tab_map),
                    pl.BlockSpec((None, G_B, CHUNK, CHUNK), gate_map),
                    pl.BlockSpec((None, G_B, CHUNK, LANES), gate_map)],
        out_specs=[heads(H_A), heads(H_A), heads(H_A // 2), row(D_C), row(LANES), row(D_B), row(D_B)],
        out_shape=outs,
        compiler_params=_params(1),
        name="even_in",
    )(xp, xs, *weights, tab, prep["wgate"], prep["bgate"])


def _mla_prompt_kernel(q_ref, k_ref, v_ref, o_ref, *, tq):
    seq = q_ref.shape[0]
    h = pl.program_id(1)
    lane = lax.broadcasted_iota(jnp.int32, (tq, LANES), 1)
    row = lax.broadcasted_iota(jnp.int32, (tq, tq), 0)
    col = lax.broadcasted_iota(jnp.int32, (tq, tq), 1)
    diag_mask = col <= row
    for i in range(seq // tq):
        cur = slice(i * tq, (i + 1) * tq)
        past = slice(0, i * tq)
        qi = q_ref[cur, :]
        s_d = jnp.where(diag_mask, _dot_nt(qi, k_ref[cur, :]), NEG)
        m = jnp.max(s_d, -1, keepdims=True)
        if i:
            s_p = _dot_nt(qi, k_ref[past, :])
            m = jnp.maximum(m, jnp.max(s_p, -1, keepdims=True))
        p_d = jnp.exp(s_d - m)
        l = jnp.sum(p_d, -1, keepdims=True)
        acc = _dot(p_d.astype(BF16), v_ref[cur, :])
        if i:
            p_p = jnp.exp(s_p - m)
            l = l + jnp.sum(p_p, -1, keepdims=True)
            acc = acc + _dot(p_p.astype(BF16), v_ref[past, :])
        o = (acc / l).astype(BF16)

        @pl.when(h % 2 == 0)
        def _():
            o_ref[cur, :] = o

        @pl.when(h % 2 == 1)
        def _():
            o_ref[cur, :] = jnp.where(lane < DV_A, o_ref[cur, :], o)


def _mla_prompt(q, k, v, n_batch, seq, tq):
    per_head = lambda div: pl.BlockSpec((None, seq, LANES), lambda b, h: (h // div, b, 0))
    return pl.pallas_call(
        functools.partial(_mla_prompt_kernel, tq=tq),
        grid=(n_batch, H_A),
        in_specs=[per_head(1), per_head(1), per_head(2)],
        out_specs=per_head(2),
        out_shape=jax.ShapeDtypeStruct((H_A // 2, n_batch * seq, LANES), BF16),
        compiler_params=_params(2),
        name="mla_prompt",
    )(q, k, v)


Q_LAT_W = D_C + LANES


def _mla_sample_kernel(pt_ref, q_ref, ckvn_ref, kpen_ref, wq_ref, wuv_ref, ckv_hbm, kpe_hbm, o_ref,
                       ckv_buf, kpe_buf, sem, *, n_pages, t_dec):
    b = pl.program_id(0)
    rows = H_A * t_dec

    def page_copies(seq, slot, p):
        page = pt_ref[seq * n_pages + p]
        return (pltpu.make_async_copy(ckv_hbm.at[0, page], ckv_buf.at[slot, p], sem.at[0, slot]),
                pltpu.make_async_copy(kpe_hbm.at[0, page], kpe_buf.at[slot, p], sem.at[1, slot]))

    def fetch(seq, slot):
        def body(p, carry):
            for cp in page_copies(seq, slot, p):
                cp.start()
            return carry
        lax.fori_loop(0, n_pages, body, 0)

    @pl.when(b == 0)
    def _():
        fetch(0, 0)

    @pl.when(b + 1 < pl.num_programs(0))
    def _():
        fetch(b + 1, (b + 1) % 2)

    slot = b % 2

    def wait_body(p, carry):
        for cp in page_copies(b, slot, p):
            cp.wait()
        return carry
    lax.fori_loop(0, n_pages, wait_body, 0)

    q = q_ref[...]
    qs = jnp.concatenate([q[:, h * LANES:(h + 1) * LANES] for h in range(H_A)], axis=0).astype(BF16)
    full = _dot(qs, wq_ref[...])
    ql = jnp.concatenate(
        [full[h * t_dec:(h + 1) * t_dec, h * Q_LAT_W:(h + 1) * Q_LAT_W] for h in range(H_A)], axis=0)
    qlat = ql[:, :D_C].astype(BF16)
    qrope = ql[:, D_C:D_C + DR_A].astype(BF16)

    def partial_softmax(s, vals):
        m = jnp.max(s, -1, keepdims=True)
        p = jnp.exp(s - m)
        return m, jnp.sum(p, -1, keepdims=True), _dot(p.astype(BF16), vals)

    ck = ckv_buf[slot].reshape(n_pages * PAGE_SIZE, D_C).astype(BF16)
    kt = jnp.concatenate([kpe_buf[slot, p].astype(BF16) for p in range(n_pages)], axis=1)
    stats = [partial_softmax(_dot_nt(qlat, ck) + _dot(qrope, kt), ck)]

    pad = PAGE_SIZE - t_dec
    cn = jnp.concatenate([ckvn_ref[...], jnp.zeros((pad, D_C), F32)], axis=0).astype(BF16)
    kn = jnp.concatenate([kpen_ref[...][:, :DR_A], jnp.zeros((pad, DR_A), F32)], axis=0).astype(BF16)
    r = lax.broadcasted_iota(jnp.int32, (rows, PAGE_SIZE), 0) % t_dec
    cc = lax.broadcasted_iota(jnp.int32, (rows, PAGE_SIZE), 1)
    stats.append(partial_softmax(jnp.where(cc <= r, _dot_nt(qlat, cn) + _dot_nt(qrope, kn), NEG), cn))

    m = functools.reduce(jnp.maximum, [st[0] for st in stats])
    scales = [jnp.exp(st[0] - m) for st in stats]
    l = sum(sc * st[1] for sc, st in zip(scales, stats))
    acc = sum(sc * st[2] for sc, st in zip(scales, stats))
    o_lat = (acc / l).astype(BF16)
    full = _dot(o_lat, wuv_ref[...])
    rh = lax.broadcasted_iota(jnp.int32, full.shape, 0) // t_dec
    ch = lax.broadcasted_iota(jnp.int32, full.shape, 1) // DV_A
    full = jnp.where(rh == ch, full, 0.0)
    out = full[0:t_dec]
    for h in range(1, H_A):
        out = out + full[h * t_dec:(h + 1) * t_dec]
    o_ref[...] = out


def _mla_sample(page_table, q_s, ckv_new, kpe_new, wq2, wuv, cache_ckv, cache_kpe_t):
    nb, t_dec = q_s.shape[:2]
    n_pages = page_table.shape[1]

    per_b = lambda w: pl.BlockSpec((None, t_dec, w), lambda b, pt: (b, 0, 0))
    grid_spec = pltpu.PrefetchScalarGridSpec(
        num_scalar_prefetch=1,
        grid=(nb,),
        in_specs=[per_b(H_A * LANES), per_b(D_C), per_b(LANES),
                  pl.BlockSpec(wq2.shape, lambda b, pt: (0, 0)),
                  pl.BlockSpec(wuv.shape, lambda b, pt: (0, 0)),
                  pl.BlockSpec(memory_space=pl.ANY), pl.BlockSpec(memory_space=pl.ANY)],
        out_specs=per_b(H_A * DV_A),
        scratch_shapes=[pltpu.VMEM((2, n_pages, PAGE_SIZE, D_C), F32),
                        pltpu.VMEM((2, n_pages, DR_A, PAGE_SIZE), F32),
                        pltpu.SemaphoreType.DMA((2, 2))],
    )
    return pl.pallas_call(
        functools.partial(_mla_sample_kernel, n_pages=n_pages, t_dec=t_dec),
        grid_spec=grid_spec,
        out_shape=jax.ShapeDtypeStruct((nb, t_dec, H_A * DV_A), F32),
        compiler_params=_params(1),
        name="mla_sample",
    )(page_table.reshape(-1), q_s, ckv_new, kpe_new, wq2, wuv, cache_ckv, cache_kpe_t)


def _swiglu(xb, wg_ref, wu_ref, wd_ref):
    f = wg_ref.shape[1]
    step = 3 * MXU_DIM
    y = None
    for s in range(0, f, step):
        cs = slice(s, min(s + step, f))
        h = jax.nn.silu(_dot(xb, wg_ref[:, cs])) * _dot(xb, wu_ref[:, cs])
        t = _dot(h.astype(BF16), wd_ref[cs, :])
        y = t if y is None else y + t
    return y


def _layer0_out_kernel(xp_ref, xs_ref, ap_ref, as_ref, b_ref, wa_ref, wb_ref, g0_ref, b0_ref, wg_ref, wu_ref,
                       wd_ref, g1_ref, b1_ref, o_ref, ob_ref, *, n_prompt_tiles):
    is_prompt = pl.program_id(0) < n_prompt_tiles
    x = jnp.where(is_prompt, xp_ref[...], xs_ref[...])
    att = jnp.concatenate([jnp.where(is_prompt, ap_ref[p], as_ref[p]) for p in range(ap_ref.shape[0])], axis=1)
    mix = _dot(att, wa_ref[...]) + _dot(b_ref[...], wb_ref[...])
    x1 = _layer_norm(ALPHA * x + mix, g0_ref[...], b0_ref[...])
    y = _swiglu(x1.astype(BF16), wg_ref, wu_ref, wd_ref)
    out = _layer_norm(ALPHA * x1 + y, g1_ref[...], b1_ref[...])
    o_ref[...] = out
    ob_ref[...] = out.astype(BF16)


def _two_group_specs(tm, width, n_prompt_tiles):
    return [pl.BlockSpec((tm, width), lambda i: (jnp.minimum(i, n_prompt_tiles - 1), 0)),
            pl.BlockSpec((tm, width), lambda i: (jnp.maximum(i - n_prompt_tiles, 0), 0))]


def _layer0_out(xp, xs, a_p, a_s, b, weights, tm):
    (mp, d), ms = xp.shape, xs.shape[0]
    m = mp + ms
    n_p = mp // tm
    resident = lambda arr: pl.BlockSpec(arr.shape, lambda i: (0, 0), pipeline_mode=pl.Buffered(1))
    row = lambda width: pl.BlockSpec((tm, width), lambda i: (i, 0))
    pairs = a_p.shape[0]
    return pl.pallas_call(
        functools.partial(_layer0_out_kernel, n_prompt_tiles=n_p),
        grid=(m // tm,),
        in_specs=_two_group_specs(tm, d, n_p)
                 + [pl.BlockSpec((pairs, tm, LANES), lambda i: (0, jnp.minimum(i, n_p - 1), 0)),
                    pl.BlockSpec((pairs, tm, LANES), lambda i: (0, jnp.maximum(i - n_p, 0), 0)),
                    row(b.shape[1])]
                 + [resident(w) for w in weights],
        out_specs=[row(d), row(d)],
        out_shape=[jax.ShapeDtypeStruct((m, d), F32), jax.ShapeDtypeStruct((m, d), BF16)],
        compiler_params=_params(1),
        name="mix_out_ffn_ln",
    )(xp, xs, a_p, a_s, b, *weights)


def _odd_in_prompt_kernel(x_ref, w_ref, q_ref, k_ref, v_ref, kv_ref, scr_ref, *, d, scale, kv_rows):
    tm = x_ref.shape[0]
    n = tm // d
    x = x_ref[...]
    for part, (o_ref, s) in enumerate(((q_ref, scale), (k_ref, 1.0), (v_ref, 1.0))):
        y = _dot(x, w_ref[:, part * W_C:(part + 1) * W_C])
        if part:
            kv_ref[(part - 1) * W_C:part * W_C, :] = y[tm - kv_rows:, :].T
        if d == 1:
            o_ref[0] = (y * s).astype(BF16)
        else:
            for c in range(W_C // LANES):
                scr_ref[c] = y[:, c * LANES:(c + 1) * LANES] * s
            for r in range(d):
                o_ref[r] = jnp.concatenate(
                    [scr_ref[c, pl.ds(r, n, stride=d), :] for c in range(W_C // LANES)], axis=1).astype(BF16)


def _odd_in_prompt(xb, w_g, g, n_batch, seq, tm):
    d = C_DILATIONS[g]
    win = min(C_WINDOWS[g], seq)
    tiles = seq // tm
    assert tm % d == 0 and (tm // d) % 16 == 0
    if win >= tm:
        assert win % tm == 0
        kv_rows, first = tm, (seq - win) // tm
    else:
        assert tm % win == 0
        kv_rows, first = win, tiles - 1
    dil = pl.BlockSpec((None, d, tm // d, W_C), lambda i: (i // tiles, 0, i % tiles, 0))
    kv_spec = pl.BlockSpec((None, 2 * W_C, kv_rows),
                           lambda i: (i // tiles, 0, jnp.maximum(i % tiles - first, 0)))
    dil_shape = jax.ShapeDtypeStruct((n_batch, d, seq // d, W_C), BF16)
    return pl.pallas_call(
        functools.partial(_odd_in_prompt_kernel, d=d, scale=DH_C ** -0.5, kv_rows=kv_rows),
        grid=(n_batch * tiles,),
        in_specs=[pl.BlockSpec((tm, xb.shape[1]), lambda i: (i, 0)), _full_spec(w_g)],
        out_specs=[dil, dil, dil, kv_spec],
        out_shape=[dil_shape, dil_shape, dil_shape, jax.ShapeDtypeStruct((n_batch, 2 * W_C, win), F32)],
        scratch_shapes=[pltpu.VMEM((W_C // LANES, tm, LANES), F32)],
        compiler_params=_params(1),
        name=f"odd_in_prompt_g{g}",
    )(xb, w_g)


def _odd_in_sample_kernel(x_ref, w_ref, q_ref, kv_ref, *, scale):
    y = _dot(x_ref[...], w_ref[...])
    q_ref[...] = y[:, :W_C] * scale
    kv_ref[...] = y[:, W_C:]


def _odd_in_sample(xb, w_g, g, row0, n_rows, tm):
    return pl.pallas_call(
        functools.partial(_odd_in_sample_kernel, scale=DH_C ** -0.5),
        grid=(n_rows // tm,),
        in_specs=[pl.BlockSpec((tm, xb.shape[1]), lambda i: (row0 // tm + i, 0)), _full_spec(w_g)],
        out_specs=[pl.BlockSpec((tm, W_C), lambda i: (i, 0)), pl.BlockSpec((tm, 2 * W_C), lambda i: (i, 0))],
        out_shape=[jax.ShapeDtypeStruct((n_rows, W_C), F32), jax.ShapeDtypeStruct((n_rows, 2 * W_C), F32)],
        compiler_params=_params(1),
        name=f"odd_in_sample_g{g}",
    )(xb, w_g)


def _dsw_prompt_kernel(q_ref, k_ref, v_ref, bias_ref, o_ref, lse_ref, *, n_blk):
    blk = C_BLOCK
    lane = lax.broadcasted_iota(jnp.int32, (blk, LANES), 1)
    lo = lane < DH_C

    def block(r, n, first):
        if first:
            qs = ks = pl.ds(0, blk)
        else:
            qs = pl.ds(pl.multiple_of(n * blk, blk), blk)
            ks = pl.ds(pl.multiple_of((n - 1) * blk, blk), 2 * blk)
        parts = []
        for hp in range(H_C // 2):
            hs = slice(hp * LANES, (hp + 1) * LANES)
            qp = q_ref[r, qs, hs]
            zero = jnp.zeros_like(qp)
            qq = jnp.concatenate([jnp.where(lo, qp, zero), jnp.where(lo, zero, qp)], axis=0)
            parts.append(_dot_nt(qq, k_ref[r, ks, hs]))
        s = jnp.concatenate(parts, axis=0) + (bias_ref[:, blk:] if first else bias_ref[...])
        m = jnp.max(s, -1, keepdims=True)
        p = jnp.exp(s - m)
        l = jnp.sum(p, -1, keepdims=True)
        pn = (p * (1.0 / l)).astype(BF16)
        lse = m + jnp.log(l)
        lse_tile = jnp.zeros((blk, LANES), F32)
        for hp in range(H_C // 2):
            hs = slice(hp * LANES, (hp + 1) * LANES)
            o2 = _dot(pn[2 * hp * blk:(2 * hp + 2) * blk], v_ref[r, ks, hs])
            o_ref[r, qs, hs] = jnp.where(lo, o2[:blk], o2[blk:]).astype(BF16)
            for h in (2 * hp, 2 * hp + 1):
                lse_tile = jnp.where(lane == h, lse[h * blk:(h + 1) * blk], lse_tile)
        lse_ref[r, qs, :] = lse_tile

    def residue(r, carry):
        block(r, 0, True)
        if n_blk > 1:
            def body(n, c):
                block(r, n, False)
                return c
            lax.fori_loop(1, n_blk, body, 0)
        return carry

    lax.fori_loop(0, q_ref.shape[0], residue, 0)


def _dsw_prompt(q, k, v, bias, g):
    n_batch, d, length, w = q.shape
    assert length % C_BLOCK == 0
    spec = lambda width: pl.BlockSpec((None, d, length, width), lambda b: (b, 0, 0, 0))
    return pl.pallas_call(
        functools.partial(_dsw_prompt_kernel, n_blk=length // C_BLOCK),
        grid=(n_batch,),
        in_specs=[spec(w), spec(w), spec(w), _full_spec(bias)],
        out_specs=[spec(w), spec(LANES)],
        out_shape=[jax.ShapeDtypeStruct((n_batch, d, length, w), BF16),
                   jax.ShapeDtypeStruct((n_batch, d, length, LANES), F32)],
        compiler_params=_params(1),
        name=f"dsw_prompt_g{g}",
    )(q, k, v, bias)


def _dsw_sample_kernel(q_ref, qh_ref, kvn_ref, buf_ref, mask_ref, maskn_ref, o_ref, lse_ref, *, t_dec, n_seq):
    for i in range(n_seq):
        rs = slice(i * t_dec, (i + 1) * t_dec)
        o, lse = _dsw_sample_one(q_ref[rs, :], qh_ref.at[i], kvn_ref[rs, :], buf_ref.at[i], mask_ref, maskn_ref,
                                 t_dec)
        o_ref[rs, :] = o
        lse_ref[rs, :] = lse


def _dsw_sample_one(q, qh_ref, kvn, buf_ref, mask_ref, maskn_ref, t_dec):
    rows = H_C * t_dec
    rh = lax.broadcasted_iota(jnp.int32, (rows, W_C), 0) // t_dec
    ch = lax.broadcasted_iota(jnp.int32, (rows, W_C), 1) // DH_C
    same = rh == ch
    lane = lax.broadcasted_iota(jnp.int32, (rows, LANES), 1)
    rl = lax.broadcasted_iota(jnp.int32, (rows, LANES), 0) // t_dec
    pad = LANES - t_dec

    qbd = jnp.where(same, jnp.concatenate([q] * H_C, axis=0), 0.0).astype(BF16)
    kn = jnp.concatenate([kvn[:, :W_C], jnp.zeros((pad, W_C), F32)], axis=0).astype(BF16)
    vn = jnp.concatenate([kvn[:, W_C:], jnp.zeros((pad, W_C), F32)], axis=0).astype(BF16)
    s_n = _dot_nt(qbd, kn) + maskn_ref[...]

    s_b = jnp.concatenate(
        [_dot(qh_ref[h].astype(BF16), buf_ref[0, h].astype(BF16)) for h in range(H_C)], axis=0) + mask_ref[...]

    m = jnp.maximum(jnp.max(s_b, -1, keepdims=True), jnp.max(s_n, -1, keepdims=True))
    p_b = jnp.exp(s_b - m)
    p_n = jnp.exp(s_n - m)
    l = jnp.sum(p_b, -1, keepdims=True) + jnp.sum(p_n, -1, keepdims=True)
    p_b = p_b / l
    o_n = jnp.where(same, _dot((p_n / l).astype(BF16), vn), 0.0)
    lse = jnp.where(lane == rl, m + jnp.log(l), 0.0)
    o_acc = o_n[0:t_dec]
    lse_acc = lse[0:t_dec]
    for h in range(1, H_C):
        o_acc = o_acc + o_n[h * t_dec:(h + 1) * t_dec]
        lse_acc = lse_acc + lse[h * t_dec:(h + 1) * t_dec]
    o_b = jnp.concatenate(
        [_dot_nt(p_b[h * t_dec:(h + 1) * t_dec].astype(BF16), buf_ref[1, h].astype(BF16)) for h in range(H_C)],
        axis=1)
    return o_acc + o_b, lse_acc


DSW_SAMPLE_BLOCK_BYTES = 8 * 2**20


def _dsw_sample(q_s, q_hm, kv_new, buf_t, mask, mask_new):
    nb, _, t_dec, _ = q_hm.shape
    ms = nb * t_dec
    n_seq = max(1, min(8, DSW_SAMPLE_BLOCK_BYTES // (4 * math.prod(buf_t.shape[1:]))))
    while nb % n_seq:
        n_seq -= 1
    row = lambda width: pl.BlockSpec((n_seq * t_dec, width), lambda b: (b, 0))
    return pl.pallas_call(
        functools.partial(_dsw_sample_kernel, t_dec=t_dec, n_seq=n_seq),
        grid=(nb // n_seq,),
        in_specs=[row(W_C), pl.BlockSpec((n_seq,) + q_hm.shape[1:], lambda b: (b, 0, 0, 0)), row(2 * W_C),
                  pl.BlockSpec((n_seq,) + buf_t.shape[1:], lambda b: (b, 0, 0, 0, 0)),
                  _full_spec(mask), _full_spec(mask_new)],
        out_specs=[row(W_C), row(LANES)],
        out_shape=[jax.ShapeDtypeStruct((ms, W_C), F32), jax.ShapeDtypeStruct((ms, LANES), F32)],
        compiler_params=_params(1),
        name="dsw_sample",
    )(q_s, q_hm, kv_new, buf_t, mask, mask_new)


def _split_bf16(x):
    hi = x.astype(BF16)
    return hi, (x - hi.astype(F32)).astype(BF16)


def _interleave(refs, is_prompt, scr_ref):
    p_ref, s_ref = refs
    d, n, width = p_ref.shape
    row = lambda r: jnp.where(is_prompt, p_ref[r], s_ref[r]).astype(F32)
    if d == 1:
        return row(0)
    slabs = width // LANES
    for r in range(d):
        val = row(r)
        for c in range(slabs):
            scr_ref[c, pl.ds(r, n, stride=d), :] = val[:, c * LANES:(c + 1) * LANES]
    return jnp.concatenate([scr_ref[c] for c in range(slabs)], axis=1)


def _merge_kernel(x_ref, *refs, n_prompt_tiles):
    ng = N_GROUPS_C
    o_refs = [refs[2 * g:2 * g + 2] for g in range(ng)]
    l_refs = [refs[2 * (ng + g):2 * (ng + g) + 2] for g in range(ng)]
    e_ref, w_ref, g_ref, beta_ref, wr_hi_ref, wr_lo_ref, y_ref, yb_ref, route_ref, o_scr, l_scr = refs[4 * ng:]
    is_prompt = pl.program_id(0) < n_prompt_tiles
    lses = [_interleave(l, is_prompt, l_scr) for l in l_refs]
    mx = jnp.maximum(jnp.maximum(lses[0], lses[1]), lses[2])
    es = [jnp.exp(l - mx) for l in lses]
    den = es[0] + es[1] + es[2]
    acc = None
    for e, o in zip(es, o_refs):
        hi, lo = _split_bf16(e / den)
        wexp = _dot(hi, e_ref[...]) + _dot(lo, e_ref[...])
        term = wexp * _interleave(o, is_prompt, o_scr)
        acc = term if acc is None else acc + term
    y = _dot(acc.astype(BF16), w_ref[...])
    x2 = _layer_norm(ALPHA * x_ref[...] + y, g_ref[...], beta_ref[...])
    y_ref[...] = x2
    yb_ref[...] = x2.astype(BF16)

    xh, xl = _split_bf16(x2)
    logits = _dot(xh, wr_hi_ref[...]) + _dot(xl, wr_hi_ref[...]) + _dot(xh, wr_lo_ref[...])
    lane = lax.broadcasted_iota(jnp.int32, logits.shape, 1)
    logits = jnp.where(lane < N_EXPERTS, logits, -jnp.inf)
    v1 = jnp.max(logits, -1, keepdims=True)
    i1 = jnp.min(jnp.where(logits == v1, lane, LANES), -1, keepdims=True)
    rest_l = jnp.where(lane == i1, -jnp.inf, logits)
    v2 = jnp.max(rest_l, -1, keepdims=True)
    i2 = jnp.min(jnp.where(rest_l == v2, lane, LANES), -1, keepdims=True)
    e2 = jnp.exp(v2 - v1)
    g1 = 1.0 / (1.0 + e2)
    g2 = e2 / (1.0 + e2)
    route = jnp.where(lane == 0, i1.astype(F32), 0.0)
    route = jnp.where(lane == 1, i2.astype(F32), route)
    route = jnp.where(lane == 2, g1, route)
    route = jnp.where(lane == 3, g2, route)
    route_ref[...] = route


def _residue_major(a, d, tm):
    rows, width = a.shape
    a = a.reshape(rows // tm, tm // d, d, width)
    return jnp.transpose(a, (2, 0, 1, 3)).reshape(1, d, rows // d, width)


def _merge(x, groups, weights, tm, n_prompt_rows, tiles_per_batch):
    m, d_model = x.shape
    n_p = n_prompt_rows // tm

    def pair(prompt, sample):
        _, d, _, width = prompt.shape
        blk = (None, d, tm // d, width)

        def prompt_map(i):
            j = jnp.minimum(i, n_p - 1)
            return (j // tiles_per_batch, 0, j % tiles_per_batch, 0)

        return [pl.BlockSpec(blk, prompt_map), pl.BlockSpec(blk, lambda i: (0, 0, jnp.maximum(i - n_p, 0), 0))]

    row = lambda width: pl.BlockSpec((tm, width), lambda i: (i, 0))
    o_in = [a for g in groups for a in g[:2]]
    l_in = [a for g in groups for a in g[2:]]
    return pl.pallas_call(
        functools.partial(_merge_kernel, n_prompt_tiles=n_p),
        grid=(m // tm,),
        in_specs=[row(d_model)]
                 + [s for g in groups for s in pair(*g[:2])] + [s for g in groups for s in pair(*g[2:])]
                 + [_full_spec(a) for a in weights],
        out_specs=[row(d_model), row(d_model), row(LANES)],
        out_shape=[jax.ShapeDtypeStruct((m, d_model), F32), jax.ShapeDtypeStruct((m, d_model), BF16),
                   jax.ShapeDtypeStruct((m, LANES), F32)],
        scratch_shapes=[pltpu.VMEM((W_C // LANES, tm, LANES), F32), pltpu.VMEM((1, tm, LANES), F32)],
        compiler_params=_params(1),
        name="merge_out_ln_route",
    )(x, *o_in, *l_in, *weights)


def _moe_kernel(te_ref, nt_ref, x_ref, wg_ref, wu_ref, wd_ref, o_ref):
    t = pl.program_id(0)

    @pl.when(t < nt_ref[0])
    def _():
        o_ref[...] = _swiglu(x_ref[...], wg_ref, wu_ref, wd_ref)

    @pl.when(t >= nt_ref[0])
    def _():
        o_ref[...] = jnp.zeros(o_ref.shape, F32)


def _moe(tile_expert, n_tiles_used, xs, wg, wu, wd, tm):
    p, d = xs.shape
    f = wg.shape[2]
    grid_spec = pltpu.PrefetchScalarGridSpec(
        num_scalar_prefetch=2,
        grid=(p // tm,),
        in_specs=[pl.BlockSpec((tm, d), lambda t, te, nt: (t, 0)),
                  pl.BlockSpec((None, d, f), lambda t, te, nt: (te[t], 0, 0), pipeline_mode=pl.Buffered(1)),
                  pl.BlockSpec((None, d, f), lambda t, te, nt: (te[t], 0, 0), pipeline_mode=pl.Buffered(1)),
                  pl.BlockSpec((None, f, d), lambda t, te, nt: (te[t], 0, 0), pipeline_mode=pl.Buffered(1))],
        out_specs=pl.BlockSpec((tm, d), lambda t, te, nt: (t, 0)),
    )
    return pl.pallas_call(
        _moe_kernel,
        grid_spec=grid_spec,
        out_shape=jax.ShapeDtypeStruct((p, d), F32),
        compiler_params=_params(1),
        name="moe_experts",
    )(tile_expert, n_tiles_used, xs, wg, wu, wd)


def _final_kernel(x_ref, ya_ref, yb_ref, route_ref, g_ref, beta_ref, op_ref, os_ref, *, n_prompt_tiles):
    route = route_ref[...]
    y = route[:, 2:3] * ya_ref[...] + route[:, 3:4] * yb_ref[...]
    out = _layer_norm(ALPHA * x_ref[...] + y, g_ref[...], beta_ref[...])
    i = pl.program_id(0)

    @pl.when(i < n_prompt_tiles)
    def _():
        op_ref[...] = out

    @pl.when(i >= n_prompt_tiles)
    def _():
        os_ref[...] = out


def _final(x, ya, yb, route, g, beta, tm, n_prompt_rows):
    m, d = x.shape
    n_p = n_prompt_rows // tm
    row = lambda arr: pl.BlockSpec((tm, arr.shape[1]), lambda i: (i, 0))
    return pl.pallas_call(
        functools.partial(_final_kernel, n_prompt_tiles=n_p),
        grid=(m // tm,),
        in_specs=[row(x), row(ya), row(yb), row(route), _full_spec(g), _full_spec(beta)],
        out_specs=[pl.BlockSpec((tm, d), lambda i: (jnp.minimum(i, n_p - 1), 0)),
                   pl.BlockSpec((tm, d), lambda i: (jnp.maximum(i - n_p, 0), 0))],
        out_shape=[jax.ShapeDtypeStruct((n_prompt_rows, d), F32),
                   jax.ShapeDtypeStruct((m - n_prompt_rows, d), F32)],
        compiler_params=_params(1),
        name="moe_combine_ln",
    )(x, ya, yb, route, g, beta)


def _rope_table(pos, scale):
    half = DR_A // 2
    inv = ROPE_BASE ** (-jnp.arange(half, dtype=F32) / half)
    ang = pos.astype(F32)[:, None] * inv[None, :]
    cos, sin = jnp.cos(ang), jnp.sin(ang)
    n = pos.shape[0]
    z = lambda w: jnp.zeros((n, w), F32)
    one = jnp.ones((n, DN_A), F32)
    cq = jnp.concatenate([one, cos, cos, z(32)], 1) * scale
    s1q = jnp.concatenate([z(DN_A), -sin, z(16), z(32)], 1) * scale
    s2q = jnp.concatenate([z(DN_A), z(16), sin, z(32)], 1) * scale
    ck = jnp.concatenate([cos, cos, z(96)], 1)
    s1k = jnp.concatenate([-sin, z(112)], 1)
    s2k = jnp.concatenate([z(16), sin, z(96)], 1)
    return jnp.concatenate([cq, s1q, s2q, ck, s1k, s2k], 1)


def _t5_bucket(dist):
    max_exact = N_BUCKETS // 2
    log_ratio = jnp.log(jnp.maximum(dist, 1).astype(F32) / max_exact) / math.log(MAX_DISTANCE / max_exact)
    large = jnp.minimum(max_exact + (log_ratio * (N_BUCKETS - max_exact)).astype(jnp.int32), N_BUCKETS - 1)
    return jnp.where(dist < max_exact, dist, large)


def _group_bias(rel_bias, g):
    d = C_DILATIONS[g]
    dist = d * jnp.arange(C_WINDOWS[g] // d + 1, dtype=jnp.int32)
    onehot = (_t5_bucket(dist)[:, None] == jnp.arange(N_BUCKETS)[None, :]).astype(F32)
    table = rel_bias[:, g * H_C:(g + 1) * H_C]
    return jnp.transpose(jnp.dot(onehot, table, precision=lax.Precision.HIGHEST))


def _prompt_bias(bias):
    n_back = bias.shape[1] - 1
    a = np.arange(C_BLOCK)[:, None]
    c = np.arange(2 * C_BLOCK)[None, :]
    j = a + C_BLOCK - c
    band = (j >= 0) & (j <= n_back)
    full = _select_columns(bias, np.clip(j, 0, n_back).reshape(-1))
    return jnp.where(band[None], full.reshape(-1, C_BLOCK, 2 * C_BLOCK), NEG).reshape(-1, 2 * C_BLOCK)


def _select_columns(bias, cols):
    sel = (jnp.arange(bias.shape[1], dtype=jnp.int32)[:, None] == jnp.asarray(cols, jnp.int32)[None, :])
    return jnp.dot(bias.astype(F32), sel.astype(F32), precision=lax.Precision.HIGHEST)


def _sample_masks(bias, g, n_buf, t_dec):
    d = C_DILATIONS[g]
    n_back = bias.shape[1] - 1
    t = np.arange(t_dec)[:, None]

    def build(key_pos):
        diff = n_buf + t - key_pos[None, :]
        ok = (diff >= 0) & (diff % d == 0) & (diff // d <= n_back)
        full = _select_columns(bias, np.clip(diff // d, 0, n_back).reshape(-1))
        mt = jnp.where(ok[None], full.reshape(H_C, t_dec, -1), NEG)
        return mt.reshape(H_C * t_dec, -1)

    mask_new = build(n_buf + np.arange(LANES))
    col_ok = (np.arange(LANES) < t_dec)[None, :]
    return build(np.arange(n_buf)), jnp.where(col_ok, mask_new, NEG)


def _prep_even(e_w_in, e_q_norm_g, e_kv_norm_g, e_w_uq, e_w_uk, e_w_uv, e_v_ln_g, e_v_ln_b, e_w_s, e_b_s, t_dec):
    o_kr = D_CQ + D_C
    o_uv = o_kr + DR_A
    d_model = e_w_in.shape[0]
    w_in = jnp.concatenate([e_w_in[:, :o_kr], e_w_in[:, o_uv:], e_w_in[:, o_kr:o_uv],
                            jnp.zeros((d_model, LANES - DR_A), F32)], axis=1).astype(BF16)
    w_uq = jnp.pad(e_w_uq, ((0, 0), (0, 0), (0, LANES - DN_A - DR_A))).reshape(D_CQ, H_A * LANES).astype(BF16)
    w_uk_pad = jnp.pad(e_w_uk, ((0, 0), (0, 0), (0, LANES - DN_A))).reshape(D_C, H_A * LANES)
    w_uv = e_w_uv.reshape(D_C, H_A * DV_A)
    place = np.zeros((LANES, H_A, LANES), np.float32)
    for i in range(DR_A):
        place[i, :, DN_A + i] = 1.0
    place = place.reshape(LANES, H_A * LANES)
    w_kv = jnp.concatenate([
        jnp.concatenate([w_uk_pad, w_uv], axis=1),
        jnp.concatenate([jnp.asarray(place), jnp.zeros((LANES, H_A * DV_A), F32)], axis=1)], axis=0).astype(BF16)

    tril = np.tril(np.ones((CHUNK, CHUNK), bool))
    wg_p = jnp.where(tril[None], e_w_s, 0.0)
    small = jnp.where(np.tril(np.ones((t_dec, t_dec), bool))[None], e_w_s[:, :t_dec, :t_dec], 0.0)
    eye = jnp.eye(CHUNK // t_dec, dtype=F32)
    wg_s = jnp.einsum("ab,gts->gatbs", eye, small).reshape(G_B, CHUNK, CHUNK)
    wgate = jnp.stack([wg_p, wg_s]).astype(BF16)
    bg_p = jnp.broadcast_to(e_b_s[:, :, None], (G_B, CHUNK, LANES))
    bg_s = jnp.broadcast_to(jnp.tile(e_b_s[:, :t_dec], (1, CHUNK // t_dec))[:, :, None], (G_B, CHUNK, LANES))
    bgate = jnp.stack([bg_p, bg_s]).astype(F32)

    sel = np.zeros((LANES, Q_LAT_W), np.float32)
    for i in range(DR_A):
        sel[DN_A + i, D_C + i] = 1.0
    wq2 = jnp.broadcast_to(jnp.asarray(sel)[:, None, :], (LANES, H_A, Q_LAT_W))
    uk_t = jnp.transpose(e_w_uk, (2, 1, 0))
    wq2 = wq2.at[:DN_A, :, :D_C].add(uk_t)
    wq2 = wq2.reshape(LANES, H_A * Q_LAT_W).astype(BF16)
    return dict(w_in=w_in, gq=e_q_norm_g.reshape(1, -1), gkv=e_kv_norm_g.reshape(1, -1), w_uq=w_uq, w_kv=w_kv,
                vg=e_v_ln_g.reshape(1, -1), vb=e_v_ln_b.reshape(1, -1), wgate=wgate, bgate=bgate,
                wq2=wq2, wuv=w_uv.astype(BF16))


def _pick_tile(m_prompt, m_sample, cap):
    t = cap
    while m_prompt % t or m_sample % t:
        t //= 2
    assert t >= CHUNK
    return t


def kernel(x_prompt, x_sample, cache_mla_ckv, cache_mla_kpe, cache_dsw_kv0, cache_dsw_kv1, cache_dsw_kv2, page_table, ln_g, ln_b, e_w_in, e_q_norm_g, e_kv_norm_g, e_w_uq, e_w_uk, e_w_uv, e_v_ln_g, e_v_ln_b, e_w_s, e_b_s, e_w_out, ffn_w_gate, ffn_w_up, ffn_w_down, o_w_in, o_w_out, rel_bias, moe_w_router, moe_w_gate, moe_w_up, moe_w_down):
    nb_p, seq, d_model = x_prompt.shape
    nb_s, t_dec, _ = x_sample.shape
    past_len = page_table.shape[1] * PAGE_SIZE
    mp, ms = nb_p * seq, nb_s * t_dec
    m = mp + ms
    tm = _pick_tile(seq, ms, 512)
    tm_p = _pick_tile(seq, seq, 512)
    tm_s = _pick_tile(ms, ms, 512)
    assert CHUNK % t_dec == 0 and seq % CHUNK == 0 and mp % tm_s == 0

    xp0, xs0 = x_prompt.reshape(mp, d_model), x_sample.reshape(ms, d_model)
    ln = lambda l, j: (ln_g[l, j].reshape(1, -1), ln_b[l, j].reshape(1, -1))

    prep = _prep_even(e_w_in[0], e_q_norm_g[0], e_kv_norm_g[0], e_w_uq[0], e_w_uk[0], e_w_uv[0], e_v_ln_g[0],
                      e_v_ln_b[0], e_w_s[0], e_b_s[0], t_dec)
    scale = (DN_A + DR_A) ** -0.5
    pos = jnp.concatenate([jnp.arange(seq, dtype=jnp.int32),
                           past_len + (jnp.arange(tm, dtype=jnp.int32) % t_dec)])
    tab = _rope_table(pos, scale)
    q, k, v, ckv, kpe, gated, vln = _even_in(xp0, xs0, prep, tab, tm, seq // tm)

    att_p = _mla_prompt(q, k, v, nb_p, seq, min(256, seq))
    q_s = jnp.transpose(q[:, mp:].astype(F32), (1, 0, 2)).reshape(nb_s, t_dec, H_A * LANES)
    ckv_s = ckv[mp:].reshape(nb_s, t_dec, D_C)
    kpe_s = kpe[mp:].reshape(nb_s, t_dec, LANES)
    att_s = _mla_sample(page_table, q_s, ckv_s, kpe_s, prep["wq2"], prep["wuv"], cache_mla_ckv,
                        jnp.swapaxes(cache_mla_kpe, 2, 3))
    att_s = jnp.transpose(att_s.reshape(ms, H_A // 2, LANES), (1, 0, 2)).astype(BF16)

    w_out_e = e_w_out[0].astype(BF16)
    n_att = H_A * DV_A
    x1, x1b = _layer0_out(xp0, xs0, att_p, att_s, gated,
                          [w_out_e[:n_att], w_out_e[n_att:], *ln(0, 0), ffn_w_gate[0].astype(BF16),
                           ffn_w_up[0].astype(BF16), ffn_w_down[0].astype(BF16), *ln(0, 1)], tm)

    ng = N_GROUPS_C
    w_in_o = o_w_in[0].reshape(d_model, ng, 3 * W_C).astype(BF16)
    caches = (cache_dsw_kv0, cache_dsw_kv1, cache_dsw_kv2)
    groups, dsw_p, dsw_s = [], [], []
    for g in range(ng):
        w_g = w_in_o[:, g]
        bias = _group_bias(rel_bias, g)
        qd, kd, vd, kv_tail = _odd_in_prompt(x1b, w_g, g, nb_p, seq, tm_p)
        win = min(C_WINDOWS[g], seq)
        dsw_p.append(jnp.transpose(kv_tail.reshape(nb_p, 2, H_C, DH_C, win), (0, 4, 1, 2, 3))[None])
        o_p, lse_p = _dsw_prompt(qd, kd, vd, _prompt_bias(bias), g)

        q_new, kv_new = _odd_in_sample(x1b, w_g, g, mp, ms, tm_s)
        dsw_s.append(kv_new.reshape(1, nb_s, t_dec, 2, H_C, DH_C))
        n_buf = caches[g].shape[2]
        buf_t = jnp.transpose(caches[g][0], (0, 2, 3, 4, 1))
        q_hm = jnp.transpose(q_new.reshape(nb_s, t_dec, H_C, DH_C), (0, 2, 1, 3))
        mask, mask_new = _sample_masks(bias, g, n_buf, t_dec)
        o_s, lse_s = _dsw_sample(q_new, q_hm, kv_new, buf_t, mask, mask_new)
        d = C_DILATIONS[g]
        groups.append((o_p, _residue_major(o_s.astype(BF16), d, tm), lse_p, _residue_major(lse_s, d, tm)))

    expand = np.zeros((LANES, W_C), np.float32)
    for h in range(H_C):
        expand[h, h * DH_C:(h + 1) * DH_C] = 1.0
    wr = jnp.pad(moe_w_router[0], ((0, 0), (0, LANES - N_EXPERTS)))
    wr_hi, wr_lo = _split_bf16(wr)
    weights = [jnp.asarray(expand, BF16), o_w_out[0].astype(BF16), *ln(1, 0), wr_hi, wr_lo]
    x2, x2b, route = _merge(x1, groups, weights, tm, mp, seq // tm)

    tmoe = 512
    idx = route[:, :TOP_K].astype(jnp.int32)
    flat_e = idx.reshape(-1)
    onehot = (flat_e[:, None] == jnp.arange(N_EXPERTS)[None, :]).astype(jnp.int32)
    csum = jnp.cumsum(onehot, axis=0)
    counts = csum[-1]
    rank = jnp.sum(csum * onehot, axis=1) - 1
    padded = ((counts + tmoe - 1) // tmoe) * tmoe
    ends = jnp.cumsum(padded)
    starts = ends - padded
    pos_sorted = jnp.sum(starts[None, :] * onehot, axis=1) + rank
    p_rows = ((TOP_K * m + tmoe - 1) // tmoe + N_EXPERTS) * tmoe
    n_tiles = p_rows // tmoe
    tile_start = jnp.arange(n_tiles, dtype=jnp.int32) * tmoe
    tile_expert = jnp.minimum(jnp.sum(tile_start[:, None] >= ends[None, :], axis=1), N_EXPERTS - 1).astype(jnp.int32)
    n_used = (ends[-1] // tmoe).astype(jnp.int32).reshape(1)
    order = jnp.argsort(flat_e, stable=True).astype(jnp.int32)
    shift = starts - (jnp.cumsum(counts) - counts)
    row_shift = jnp.repeat(jnp.sum(shift[None, :] * (tile_expert[:, None] == jnp.arange(N_EXPERTS)[None, :]), axis=1),
                           tmoe)
    q_idx = jnp.clip(jnp.arange(p_rows, dtype=jnp.int32) - row_shift, 0, TOP_K * m - 1)
    src = order.at[q_idx].get(mode="promise_in_bounds") // TOP_K
    xs = x2b.at[src].get(mode="promise_in_bounds")
    ys = _moe(tile_expert, n_used, xs, moe_w_gate[0].astype(BF16), moe_w_up[0].astype(BF16),
              moe_w_down[0].astype(BF16), tmoe)
    pos2 = pos_sorted.reshape(m, TOP_K)
    ya = ys.at[pos2[:, 0]].get(mode="promise_in_bounds")
    yb = ys.at[pos2[:, 1]].get(mode="promise_in_bounds")
    y_p, y_s = _final(x2, ya, yb, route, *ln(1, 1), tm, mp)

    y_prompt = y_p.reshape(nb_p, seq, d_model)
    y_sample = y_s.reshape(nb_s, t_dec, d_model)
    n_pg = seq // PAGE_SIZE
    new_ckv_prompt = ckv[:mp].reshape(1, nb_p, n_pg, PAGE_SIZE, D_C)
    new_kpe_prompt = kpe[:mp, :DR_A].reshape(1, nb_p, n_pg, PAGE_SIZE, DR_A)
    new_ckv_sample = ckv_s.reshape(1, nb_s, t_dec, D_C)
    new_kpe_sample = kpe_s[:, :, :DR_A].reshape(1, nb_s, t_dec, DR_A)
    new_gate_v_sample = vln[mp:].reshape(1, nb_s, t_dec, D_B)
    return (y_prompt, y_sample, new_ckv_prompt, new_kpe_prompt, new_ckv_sample, new_kpe_sample, new_gate_v_sample,
            dsw_p[0], dsw_p[1], dsw_p[2], dsw_s[0], dsw_s[1], dsw_s[2])
```

```python
import functools
import math

import numpy as np
import jax
import jax.numpy as jnp
from jax import lax
from jax.experimental import pallas as pl
from jax.experimental.pallas import tpu as pltpu

BF16 = jnp.bfloat16
F32 = jnp.float32

H_A, DN_A, DR_A, DV_A = 8, 64, 32, 64
D_CQ, D_C = 384, 256
ROPE_BASE = 10000.0
CHUNK, G_B, D_B = 128, 4, 512
C_WINDOWS = (128, 512, 2048)
C_DILATIONS = (1, 4, 16)
N_GROUPS_C = 3
H_C, DH_C = 16, 64
W_C = H_C * DH_C
C_BLOCK = 128
N_BUCKETS, MAX_DISTANCE = 32, 2048
N_EXPERTS, TOP_K = 8, 2
DEPTH = 2
ALPHA = (2 * DEPTH) ** 0.25
LN_EPS = 1e-5
RMS_EPS = 1e-6
PAGE_SIZE = 128
NEG = -1e30

LANES = 128
MXU_DIM = 256
V7X_VMEM_BYTES = 64 * 2**20
VMEM_LIMIT = 56 * 2**20


def _dot(a, b):
    return jnp.dot(a, b, preferred_element_type=F32)


def _dot_nt(a, b):
    return lax.dot_general(a, b, (((1,), (1,)), ((), ())), preferred_element_type=F32)


def _params(n_axes, vmem=VMEM_LIMIT):
    return pltpu.CompilerParams(dimension_semantics=("arbitrary",) * n_axes, vmem_limit_bytes=vmem)


def _layer_norm(x, g, b):
    mu = jnp.mean(x, -1, keepdims=True)
    xc = x - mu
    var = jnp.mean(xc * xc, -1, keepdims=True)
    return xc * lax.rsqrt(var + LN_EPS) * g + b


def _rms_norm(x, g):
    return x * lax.rsqrt(jnp.mean(x * x, -1, keepdims=True) + RMS_EPS) * g


def _full_spec(a):
    return pl.BlockSpec(a.shape, lambda *_: (0,) * a.ndim)


CAST_BLOCK_BYTES = 4 * 2**20
BF16_SUBLANES = 16


def _cast_kernel(x_ref, o_ref):
    o_ref[...] = x_ref[...].astype(o_ref.dtype)


def _to_bf16(w):
    shape = w.shape
    w2 = w.reshape(-1, shape[-1])
    rows, cols = w2.shape
    blk = min(rows, max(BF16_SUBLANES, CAST_BLOCK_BYTES // (4 * cols)))
    while rows % blk or blk % BF16_SUBLANES:
        blk -= 1
    spec = pl.BlockSpec((blk, cols), lambda i: (i, 0))
    out = pl.pallas_call(
        _cast_kernel,
        grid=(rows // blk,),
        in_specs=[spec],
        out_specs=spec,
        out_shape=jax.ShapeDtypeStruct((rows, cols), BF16),
        compiler_params=_params(1),
        name="cast_bf16",
    )(w2)
    return out.reshape(shape)


def _rope_apply(x, c, s1, s2):
    return x * c + pltpu.roll(x, LANES - 16, 1) * s1 + pltpu.roll(x, 16, 1) * s2


def _even_in_kernel(xp_ref, xs_ref, w_in_ref, gq_ref, gkv_ref, w_uq_ref, w_kv_ref, vg_ref, vb_ref, tab_ref,
                    wgate_ref, bgate_ref,
                    q_ref, k_ref, v_ref, gated_ref, ckvp_ref, kpep_ref, ckvs_ref, kpes_ref, vlns_ref, *,
                    n_prompt_tiles):
    tm = xp_ref.shape[0]
    is_prompt = pl.program_id(0) < n_prompt_tiles
    x = jnp.where(is_prompt, xp_ref[...], xs_ref[...])
    h = _dot(x.astype(BF16), w_in_ref[...])
    c_q = h[:, :D_CQ]
    c_kv = h[:, D_CQ:D_CQ + D_C]
    o_u = D_CQ + D_C
    u = jax.nn.gelu(h[:, o_u:o_u + D_B])
    vv = jax.nn.gelu(h[:, o_u + D_B:o_u + 2 * D_B])
    kr = h[:, o_u + 2 * D_B:]

    tab = tab_ref[...]
    cq, s1q, s2q = tab[:, 0:128], tab[:, 128:256], tab[:, 256:384]
    ck, s1k, s2k = tab[:, 384:512], tab[:, 512:640], tab[:, 640:768]

    q = _dot(_rms_norm(c_q, gq_ref[...]).astype(BF16), w_uq_ref[...])
    for hb in range(H_A):
        q_ref[hb] = _rope_apply(q[:, hb * LANES:(hb + 1) * LANES], cq, s1q, s2q).astype(BF16)

    ckv = _rms_norm(c_kv, gkv_ref[...])
    kpe = _rope_apply(kr, ck, s1k, s2k)
    vln = _layer_norm(vv, vg_ref[...], vb_ref[...])

    @pl.when(is_prompt)
    def _():
        ckvp_ref[...] = ckv
        kpep_ref[...] = kpe

    @pl.when(jnp.logical_not(is_prompt))
    def _():
        ckvs_ref[...] = ckv
        kpes_ref[...] = kpe
        vlns_ref[...] = vln

    kv = _dot(jnp.concatenate([ckv.astype(BF16), kpe.astype(BF16)], axis=1), w_kv_ref[...])
    for hb in range(H_A):
        k_ref[hb] = kv[:, hb * LANES:(hb + 1) * LANES].astype(BF16)
    for p in range(H_A // 2):
        v_ref[p] = kv[:, (H_A + p) * LANES:(H_A + p + 1) * LANES].astype(BF16)

    vb16 = vln.astype(BF16)
    gw = D_B // G_B
    for c in range(tm // CHUNK):
        rs = slice(c * CHUNK, (c + 1) * CHUNK)
        for g in range(G_B):
            cs = slice(g * gw, (g + 1) * gw)
            f = _dot(wgate_ref[g], vb16[rs, cs]) + bgate_ref[g]
            gated_ref[rs, cs] = (u[rs, cs] * f).astype(BF16)


def _even_in(xp, xs, prep, tab, tm, tab_blocks):
    (mp, d_model), ms = xp.shape, xs.shape[0]
    m = mp + ms
    n_prompt_tiles = mp // tm

    def tab_map(i):
        return (jnp.where(i < n_prompt_tiles, i % tab_blocks, tab_blocks), 0)

    def gate_map(i):
        return (jnp.where(i < n_prompt_tiles, 0, 1), 0, 0, 0)

    row = lambda w: pl.BlockSpec((tm, w), lambda i: (i, 0))
    heads = lambda n: pl.BlockSpec((n, tm, LANES), lambda i: (0, i, 0))
    outs = [
        jax.ShapeDtypeStruct((H_A, m, LANES), BF16),
        jax.ShapeDtypeStruct((H_A, m, LANES), BF16),
        jax.ShapeDtypeStruct((H_A // 2, m, LANES), BF16),
        jax.ShapeDtypeStruct((m, D_B), BF16),
        jax.ShapeDtypeStruct((mp, D_C), F32),
        jax.ShapeDtypeStruct((mp, LANES), F32),
        jax.ShapeDtypeStruct((ms, D_C), F32),
        jax.ShapeDtypeStruct((ms, LANES), F32),
        jax.ShapeDtypeStruct((ms, D_B), F32),
    ]
    p_rows = lambda w: _two_group_specs(tm, w, n_prompt_tiles)[0]
    s_rows = lambda w: _two_group_specs(tm, w, n_prompt_tiles)[1]
    weights = [prep[n] for n in ("w_in", "gq", "gkv", "w_uq", "w_kv", "vg", "vb")]
    return pl.pallas_call(
        functools.partial(_even_in_kernel, n_prompt_tiles=n_prompt_tiles),
        grid=(m // tm,),
        in_specs=_two_group_specs(tm, d_model, n_prompt_tiles) + [_full_spec(a) for a in weights]
                 + [pl.BlockSpec((tm, 6 * LANES), tab_map),
                    pl.BlockSpec((None, G_B, CHUNK, CHUNK), gate_map),
                    pl.BlockSpec((None, G_B, CHUNK, LANES), gate_map)],
        out_specs=[heads(H_A), heads(H_A), heads(H_A // 2), row(D_B),
                   p_rows(D_C), p_rows(LANES), s_rows(D_C), s_rows(LANES), s_rows(D_B)],
        out_shape=outs,
        compiler_params=_params(1),
        name="even_in",
    )(xp, xs, *weights, tab, prep["wgate"], prep["bgate"])


def _mla_prompt_kernel(q_ref, k_ref, v_ref, o_ref, *, tq):
    seq = q_ref.shape[0]
    h = pl.program_id(1)
    lane = lax.broadcasted_iota(jnp.int32, (tq, LANES), 1)
    row = lax.broadcasted_iota(jnp.int32, (tq, tq), 0)
    col = lax.broadcasted_iota(jnp.int32, (tq, tq), 1)
    diag_mask = col <= row
    for i in range(seq // tq):
        cur = slice(i * tq, (i + 1) * tq)
        past = slice(0, i * tq)
        qi = q_ref[cur, :]
        s_d = jnp.where(diag_mask, _dot_nt(qi, k_ref[cur, :]), NEG)
        m = jnp.max(s_d, -1, keepdims=True)
        if i:
            s_p = _dot_nt(qi, k_ref[past, :])
            m = jnp.maximum(m, jnp.max(s_p, -1, keepdims=True))
        p_d = jnp.exp(s_d - m)
        l = jnp.sum(p_d, -1, keepdims=True)
        acc = _dot(p_d.astype(BF16), v_ref[cur, :])
        if i:
            p_p = jnp.exp(s_p - m)
            l = l + jnp.sum(p_p, -1, keepdims=True)
            acc = acc + _dot(p_p.astype(BF16), v_ref[past, :])
        o = (acc / l).astype(BF16)

        @pl.when(h % 2 == 0)
        def _():
            o_ref[cur, :] = o

        @pl.when(h % 2 == 1)
        def _():
            o_ref[cur, :] = jnp.where(lane < DV_A, o_ref[cur, :], o)


def _mla_prompt(q, k, v, n_batch, seq, tq):
    per_head = lambda div: pl.BlockSpec((None, seq, LANES), lambda b, h: (h // div, b, 0))
    return pl.pallas_call(
        functools.partial(_mla_prompt_kernel, tq=tq),
        grid=(n_batch, H_A),
        in_specs=[per_head(1), per_head(1), per_head(2)],
        out_specs=per_head(2),
        out_shape=jax.ShapeDtypeStruct((H_A // 2, n_batch * seq, LANES), BF16),
        compiler_params=_params(2),
        name="mla_prompt",
    )(q, k, v)


Q_LAT_W = D_C + LANES
PAGE_LOOP_UNROLL = 8


def _mla_sample_kernel(pt_ref, q_ref, ckvn_ref, kpen_ref, wq_ref, wuv_ref, ckv_hbm, kpe_hbm, o_ref,
                       ckv_buf, kpe_buf, sem, *, n_pages, t_dec):
    b = pl.program_id(0)
    rows = H_A * t_dec

    def page_copies(seq, slot, p):
        page = pt_ref[seq * n_pages + p]
        return (pltpu.make_async_copy(ckv_hbm.at[0, page], ckv_buf.at[slot, p], sem.at[0, slot]),
                pltpu.make_async_copy(kpe_hbm.at[0, page], kpe_buf.at[slot, p], sem.at[1, slot]))

    def fetch(seq, slot):
        def body(p, carry):
            for cp in page_copies(seq, slot, p):
                cp.start()
            return carry
        lax.fori_loop(0, n_pages, body, 0, unroll=PAGE_LOOP_UNROLL)

    @pl.when(b == 0)
    def _():
        fetch(0, 0)

    @pl.when(b + 1 < pl.num_programs(0))
    def _():
        fetch(b + 1, (b + 1) % 2)

    slot = b % 2

    def wait_body(p, carry):
        for cp in page_copies(b, slot, p):
            cp.wait()
        return carry
    lax.fori_loop(0, n_pages, wait_body, 0, unroll=PAGE_LOOP_UNROLL)

    q = q_ref[...]
    qs = jnp.concatenate([q[:, h * LANES:(h + 1) * LANES] for h in range(H_A)], axis=0).astype(BF16)
    full = _dot(qs, wq_ref[...])
    ql = jnp.concatenate(
        [full[h * t_dec:(h + 1) * t_dec, h * Q_LAT_W:(h + 1) * Q_LAT_W] for h in range(H_A)], axis=0)
    qlat = ql[:, :D_C].astype(BF16)
    qrope = ql[:, D_C:D_C + DR_A].astype(BF16)

    def partial_softmax(s, vals):
        m = jnp.max(s, -1, keepdims=True)
        p = jnp.exp(s - m)
        return m, jnp.sum(p, -1, keepdims=True), _dot(p.astype(BF16), vals)

    ck = ckv_buf[slot].reshape(n_pages * PAGE_SIZE, D_C).astype(BF16)
    kt = jnp.concatenate([kpe_buf[slot, p].astype(BF16) for p in range(n_pages)], axis=1)
    stats = [partial_softmax(_dot_nt(qlat, ck) + _dot(qrope, kt), ck)]

    pad = PAGE_SIZE - t_dec
    cn = jnp.concatenate([ckvn_ref[...], jnp.zeros((pad, D_C), F32)], axis=0).astype(BF16)
    kn = jnp.concatenate([kpen_ref[...][:, :DR_A], jnp.zeros((pad, DR_A), F32)], axis=0).astype(BF16)
    r = lax.broadcasted_iota(jnp.int32, (rows, PAGE_SIZE), 0) % t_dec
    cc = lax.broadcasted_iota(jnp.int32, (rows, PAGE_SIZE), 1)
    stats.append(partial_softmax(jnp.where(cc <= r, _dot_nt(qlat, cn) + _dot_nt(qrope, kn), NEG), cn))

    m = functools.reduce(jnp.maximum, [st[0] for st in stats])
    scales = [jnp.exp(st[0] - m) for st in stats]
    l = sum(sc * st[1] for sc, st in zip(scales, stats))
    acc = sum(sc * st[2] for sc, st in zip(scales, stats))
    o_lat = (acc / l).astype(BF16)
    full = _dot(o_lat, wuv_ref[...])
    rh = lax.broadcasted_iota(jnp.int32, full.shape, 0) // t_dec
    ch = lax.broadcasted_iota(jnp.int32, full.shape, 1) // DV_A
    full = jnp.where(rh == ch, full, 0.0)
    out = full[0:t_dec]
    for h in range(1, H_A):
        out = out + full[h * t_dec:(h + 1) * t_dec]
    o_ref[...] = out


def _mla_sample(page_table, q_s, ckv_new, kpe_new, wq2, wuv, cache_ckv, cache_kpe_t):
    nb, t_dec = q_s.shape[:2]
    n_pages = page_table.shape[1]

    per_b = lambda w: pl.BlockSpec((None, t_dec, w), lambda b, pt: (b, 0, 0))
    grid_spec = pltpu.PrefetchScalarGridSpec(
        num_scalar_prefetch=1,
        grid=(nb,),
        in_specs=[per_b(H_A * LANES), per_b(D_C), per_b(LANES),
                  pl.BlockSpec(wq2.shape, lambda b, pt: (0, 0)),
                  pl.BlockSpec(wuv.shape, lambda b, pt: (0, 0)),
                  pl.BlockSpec(memory_space=pl.ANY), pl.BlockSpec(memory_space=pl.ANY)],
        out_specs=per_b(H_A * DV_A),
        scratch_shapes=[pltpu.VMEM((2, n_pages, PAGE_SIZE, D_C), F32),
                        pltpu.VMEM((2, n_pages, DR_A, PAGE_SIZE), F32),
                        pltpu.SemaphoreType.DMA((2, 2))],
    )
    return pl.pallas_call(
        functools.partial(_mla_sample_kernel, n_pages=n_pages, t_dec=t_dec),
        grid_spec=grid_spec,
        out_shape=jax.ShapeDtypeStruct((nb, t_dec, H_A * DV_A), F32),
        compiler_params=_params(1),
        name="mla_sample",
    )(page_table.reshape(-1), q_s, ckv_new, kpe_new, wq2, wuv, cache_ckv, cache_kpe_t)


def _swiglu(xb, wg_ref, wu_ref, wd_ref):
    f = wg_ref.shape[1]
    step = 3 * MXU_DIM
    y = None
    for s in range(0, f, step):
        cs = slice(s, min(s + step, f))
        h = jax.nn.silu(_dot(xb, wg_ref[:, cs])) * _dot(xb, wu_ref[:, cs])
        t = _dot(h.astype(BF16), wd_ref[cs, :])
        y = t if y is None else y + t
    return y


def _layer0_out_kernel(xp_ref, xs_ref, ap_ref, as_ref, b_ref, wa_ref, wb_ref, g0_ref, b0_ref, wg_ref, wu_ref,
                       wd_ref, g1_ref, b1_ref, o_ref, ob_ref, *, n_prompt_tiles):
    is_prompt = pl.program_id(0) < n_prompt_tiles
    x = jnp.where(is_prompt, xp_ref[...], xs_ref[...])
    att = jnp.concatenate([jnp.where(is_prompt, ap_ref[p], as_ref[p]) for p in range(ap_ref.shape[0])], axis=1)
    mix = _dot(att, wa_ref[...]) + _dot(b_ref[...], wb_ref[...])
    x1 = _layer_norm(ALPHA * x + mix, g0_ref[...], b0_ref[...])
    y = _swiglu(x1.astype(BF16), wg_ref, wu_ref, wd_ref)
    out = _layer_norm(ALPHA * x1 + y, g1_ref[...], b1_ref[...])
    o_ref[...] = out
    ob_ref[...] = out.astype(BF16)


def _two_group_specs(tm, width, n_prompt_tiles):
    return [pl.BlockSpec((tm, width), lambda i: (jnp.minimum(i, n_prompt_tiles - 1), 0)),
            pl.BlockSpec((tm, width), lambda i: (jnp.maximum(i - n_prompt_tiles, 0), 0))]


def _layer0_out(xp, xs, a_p, a_s, b, weights, tm):
    (mp, d), ms = xp.shape, xs.shape[0]
    m = mp + ms
    n_p = mp // tm
    resident = lambda arr: pl.BlockSpec(arr.shape, lambda i: (0, 0), pipeline_mode=pl.Buffered(1))
    row = lambda width: pl.BlockSpec((tm, width), lambda i: (i, 0))
    pairs = a_p.shape[0]
    return pl.pallas_call(
        functools.partial(_layer0_out_kernel, n_prompt_tiles=n_p),
        grid=(m // tm,),
        in_specs=_two_group_specs(tm, d, n_p)
                 + [pl.BlockSpec((pairs, tm, LANES), lambda i: (0, jnp.minimum(i, n_p - 1), 0)),
                    pl.BlockSpec((pairs, tm, LANES), lambda i: (0, jnp.maximum(i - n_p, 0), 0)),
                    row(b.shape[1])]
                 + [resident(w) for w in weights],
        out_specs=[row(d), row(d)],
        out_shape=[jax.ShapeDtypeStruct((m, d), F32), jax.ShapeDtypeStruct((m, d), BF16)],
        compiler_params=_params(1),
        name="mix_out_ffn_ln",
    )(xp, xs, a_p, a_s, b, *weights)


def _odd_in_prompt_kernel(x_ref, w_ref, q_ref, k_ref, v_ref, kv_ref, scr_ref, *, d, scale, kv_rows):
    tm = x_ref.shape[0]
    n = tm // d
    x = x_ref[...]
    for part, (o_ref, s) in enumerate(((q_ref, scale), (k_ref, 1.0), (v_ref, 1.0))):
        y = _dot(x, w_ref[:, part * W_C:(part + 1) * W_C])
        if part:
            kv_ref[(part - 1) * W_C:part * W_C, :] = y[tm - kv_rows:, :].T
        if d == 1:
            o_ref[0] = (y * s).astype(BF16)
        else:
            for c in range(W_C // LANES):
                scr_ref[c] = y[:, c * LANES:(c + 1) * LANES] * s
            for r in range(d):
                o_ref[r] = jnp.concatenate(
                    [scr_ref[c, pl.ds(r, n, stride=d), :] for c in range(W_C // LANES)], axis=1).astype(BF16)


def _odd_in_prompt(xb, w_g, g, n_batch, seq, tm):
    d = C_DILATIONS[g]
    win = min(C_WINDOWS[g], seq)
    tiles = seq // tm
    assert tm % d == 0 and (tm // d) % 16 == 0
    if win >= tm:
        assert win % tm == 0
        kv_rows, first = tm, (seq - win) // tm
    else:
        assert tm % win == 0
        kv_rows, first = win, tiles - 1
    dil = pl.BlockSpec((None, d, tm // d, W_C), lambda i: (i // tiles, 0, i % tiles, 0))
    kv_spec = pl.BlockSpec((None, 2 * W_C, kv_rows),
                           lambda i: (i // tiles, 0, jnp.maximum(i % tiles - first, 0)))
    dil_shape = jax.ShapeDtypeStruct((n_batch, d, seq // d, W_C), BF16)
    return pl.pallas_call(
        functools.partial(_odd_in_prompt_kernel, d=d, scale=DH_C ** -0.5, kv_rows=kv_rows),
        grid=(n_batch * tiles,),
        in_specs=[pl.BlockSpec((tm, xb.shape[1]), lambda i: (i, 0)), _full_spec(w_g)],
        out_specs=[dil, dil, dil, kv_spec],
        out_shape=[dil_shape, dil_shape, dil_shape, jax.ShapeDtypeStruct((n_batch, 2 * W_C, win), F32)],
        scratch_shapes=[pltpu.VMEM((W_C // LANES, tm, LANES), F32)],
        compiler_params=_params(1),
        name=f"odd_in_prompt_g{g}",
    )(xb, w_g)


def _odd_in_sample_kernel(x_ref, w_ref, q_ref, kv_ref, *, scale):
    y = _dot(x_ref[...], w_ref[...])
    q_ref[...] = y[:, :W_C] * scale
    kv_ref[...] = y[:, W_C:]


def _odd_in_sample(xb, w_g, g, row0, n_rows, tm):
    return pl.pallas_call(
        functools.partial(_odd_in_sample_kernel, scale=DH_C ** -0.5),
        grid=(n_rows // tm,),
        in_specs=[pl.BlockSpec((tm, xb.shape[1]), lambda i: (row0 // tm + i, 0)), _full_spec(w_g)],
        out_specs=[pl.BlockSpec((tm, W_C), lambda i: (i, 0)), pl.BlockSpec((tm, 2 * W_C), lambda i: (i, 0))],
        out_shape=[jax.ShapeDtypeStruct((n_rows, W_C), F32), jax.ShapeDtypeStruct((n_rows, 2 * W_C), F32)],
        compiler_params=_params(1),
        name=f"odd_in_sample_g{g}",
    )(xb, w_g)


def _dsw_prompt_kernel(q_ref, k_ref, v_ref, bias_ref, o_ref, lse_ref, *, n_blk):
    blk = C_BLOCK
    lane = lax.broadcasted_iota(jnp.int32, (blk, LANES), 1)
    lo = lane < DH_C

    def blocks(items, first):
        row0 = lambda n: n * blk if isinstance(n, int) else pl.multiple_of(n * blk, blk)
        if first:
            sl = [(r, pl.ds(0, blk), pl.ds(0, blk)) for r, _ in items]
        else:
            sl = [(r, pl.ds(row0(n), blk), pl.ds(row0(n - 1), 2 * blk)) for r, n in items]
        scores = []
        for r, qs, ks in sl:
            parts = []
            for hp in range(H_C // 2):
                hs = slice(hp * LANES, (hp + 1) * LANES)
                qp = q_ref[r, qs, hs]
                zero = jnp.zeros_like(qp)
                qq = jnp.concatenate([jnp.where(lo, qp, zero), jnp.where(lo, zero, qp)], axis=0)
                parts.append(_dot_nt(qq, k_ref[r, ks, hs]))
            scores.append(jnp.concatenate(parts, axis=0) + (bias_ref[:, blk:] if first else bias_ref[...]))
        probs = []
        for s in scores:
            m = jnp.max(s, -1, keepdims=True)
            p = jnp.exp(s - m)
            l = jnp.sum(p, -1, keepdims=True)
            probs.append(((p * (1.0 / l)).astype(BF16), m + jnp.log(l)))
        for (r, qs, ks), (pn, lse) in zip(sl, probs):
            lse_tile = jnp.zeros((blk, LANES), F32)
            for hp in range(H_C // 2):
                hs = slice(hp * LANES, (hp + 1) * LANES)
                o2 = _dot(pn[2 * hp * blk:(2 * hp + 2) * blk], v_ref[r, ks, hs])
                o_ref[r, qs, hs] = jnp.where(lo, o2[:blk], o2[blk:]).astype(BF16)
                for h in (2 * hp, 2 * hp + 1):
                    lse_tile = jnp.where(lane == h, lse[h * blk:(h + 1) * blk], lse_tile)
            lse_ref[r, qs, :] = lse_tile

    def residue(r, carry):
        blocks([(r, 0)], True)
        if n_blk > 1:
            def body(n, c):
                blocks([(r, n)], False)
                return c
            lax.fori_loop(1, n_blk, body, 0)
        return carry

    lax.fori_loop(0, q_ref.shape[0], residue, 0)


def _dsw_prompt(q, k, v, bias, g):
    n_batch, d, length, w = q.shape
    assert length % C_BLOCK == 0
    spec = lambda width: pl.BlockSpec((None, d, length, width), lambda b: (b, 0, 0, 0))
    return pl.pallas_call(
        functools.partial(_dsw_prompt_kernel, n_blk=length // C_BLOCK),
        grid=(n_batch,),
        in_specs=[spec(w), spec(w), spec(w), _full_spec(bias)],
        out_specs=[spec(w), spec(LANES)],
        out_shape=[jax.ShapeDtypeStruct((n_batch, d, length, w), BF16),
                   jax.ShapeDtypeStruct((n_batch, d, length, LANES), F32)],
        compiler_params=_params(1),
        name=f"dsw_prompt_g{g}",
    )(q, k, v, bias)


def _dsw_sample_kernel(q_ref, qh_ref, kvn_ref, buf_ref, mask_ref, maskn_ref, o_ref, lse_ref, *, t_dec, n_seq):
    for i in range(n_seq):
        rs = slice(i * t_dec, (i + 1) * t_dec)
        o, lse = _dsw_sample_one(q_ref[rs, :], qh_ref.at[i], kvn_ref[rs, :], buf_ref.at[i], mask_ref, maskn_ref,
                                 t_dec)
        o_ref[rs, :] = o
        lse_ref[rs, :] = lse


def _dsw_sample_one(q, qh_ref, kvn, buf_ref, mask_ref, maskn_ref, t_dec):
    rows = H_C * t_dec
    rh = lax.broadcasted_iota(jnp.int32, (rows, W_C), 0) // t_dec
    ch = lax.broadcasted_iota(jnp.int32, (rows, W_C), 1) // DH_C
    same = rh == ch
    lane = lax.broadcasted_iota(jnp.int32, (rows, LANES), 1)
    rl = lax.broadcasted_iota(jnp.int32, (rows, LANES), 0) // t_dec
    pad = LANES - t_dec

    qbd = jnp.where(same, jnp.concatenate([q] * H_C, axis=0), 0.0).astype(BF16)
    kn = jnp.concatenate([kvn[:, :W_C], jnp.zeros((pad, W_C), F32)], axis=0).astype(BF16)
    vn = jnp.concatenate([kvn[:, W_C:], jnp.zeros((pad, W_C), F32)], axis=0).astype(BF16)
    s_n = _dot_nt(qbd, kn) + maskn_ref[...]

    s_b = jnp.concatenate(
        [_dot(qh_ref[h].astype(BF16), buf_ref[0, h].astype(BF16)) for h in range(H_C)], axis=0) + mask_ref[...]

    m = jnp.maximum(jnp.max(s_b, -1, keepdims=True), jnp.max(s_n, -1, keepdims=True))
    p_b = jnp.exp(s_b - m)
    p_n = jnp.exp(s_n - m)
    l = jnp.sum(p_b, -1, keepdims=True) + jnp.sum(p_n, -1, keepdims=True)
    p_b = p_b / l
    o_n = jnp.where(same, _dot((p_n / l).astype(BF16), vn), 0.0)
    lse = jnp.where(lane == rl, m + jnp.log(l), 0.0)
    o_acc = o_n[0:t_dec]
    lse_acc = lse[0:t_dec]
    for h in range(1, H_C):
        o_acc = o_acc + o_n[h * t_dec:(h + 1) * t_dec]
        lse_acc = lse_acc + lse[h * t_dec:(h + 1) * t_dec]
    o_b = jnp.concatenate(
        [_dot_nt(p_b[h * t_dec:(h + 1) * t_dec].astype(BF16), buf_ref[1, h].astype(BF16)) for h in range(H_C)],
        axis=1)
    return o_acc + o_b, lse_acc


DSW_SAMPLE_BLOCK_BYTES = 8 * 2**20


def _dsw_sample(q_s, q_hm, kv_new, buf_t, mask, mask_new):
    nb, _, t_dec, _ = q_hm.shape
    ms = nb * t_dec
    n_seq = max(1, min(8, DSW_SAMPLE_BLOCK_BYTES // (4 * math.prod(buf_t.shape[1:]))))
    while nb % n_seq:
        n_seq -= 1
    row = lambda width: pl.BlockSpec((n_seq * t_dec, width), lambda b: (b, 0))
    return pl.pallas_call(
        functools.partial(_dsw_sample_kernel, t_dec=t_dec, n_seq=n_seq),
        grid=(nb // n_seq,),
        in_specs=[row(W_C), pl.BlockSpec((n_seq,) + q_hm.shape[1:], lambda b: (b, 0, 0, 0)), row(2 * W_C),
                  pl.BlockSpec((n_seq,) + buf_t.shape[1:], lambda b: (b, 0, 0, 0, 0)),
                  _full_spec(mask), _full_spec(mask_new)],
        out_specs=[row(W_C), row(LANES)],
        out_shape=[jax.ShapeDtypeStruct((ms, W_C), F32), jax.ShapeDtypeStruct((ms, LANES), F32)],
        compiler_params=_params(1),
        name="dsw_sample",
    )(q_s, q_hm, kv_new, buf_t, mask, mask_new)


def _split_bf16(x):
    hi = x.astype(BF16)
    return hi, (x - hi.astype(F32)).astype(BF16)


def _interleave(refs, is_prompt, scr_ref):
    p_ref, s_ref = refs
    d, n, width = p_ref.shape
    row = lambda r: jnp.where(is_prompt, p_ref[r], s_ref[r]).astype(F32)
    if d == 1:
        return row(0)
    slabs = width // LANES
    for r in range(d):
        val = row(r)
        for c in range(slabs):
            scr_ref[c, pl.ds(r, n, stride=d), :] = val[:, c * LANES:(c + 1) * LANES]
    return jnp.concatenate([scr_ref[c] for c in range(slabs)], axis=1)


def _merge_kernel(x_ref, *refs, n_prompt_tiles):
    ng = N_GROUPS_C
    o_refs = [refs[2 * g:2 * g + 2] for g in range(ng)]
    l_refs = [refs[2 * (ng + g):2 * (ng + g) + 2] for g in range(ng)]
    e_ref, w_ref, g_ref, beta_ref, wr_hi_ref, wr_lo_ref, y_ref, yb_ref, route_ref, o_scr, l_scr = refs[4 * ng:]
    is_prompt = pl.program_id(0) < n_prompt_tiles
    lses = [_interleave(l, is_prompt, l_scr) for l in l_refs]
    mx = jnp.maximum(jnp.maximum(lses[0], lses[1]), lses[2])
    es = [jnp.exp(l - mx) for l in lses]
    den = es[0] + es[1] + es[2]
    acc = None
    for e, o in zip(es, o_refs):
        hi, lo = _split_bf16(e / den)
        wexp = _dot(hi, e_ref[...]) + _dot(lo, e_ref[...])
        term = wexp * _interleave(o, is_prompt, o_scr)
        acc = term if acc is None else acc + term
    y = _dot(acc.astype(BF16), w_ref[...])
    x2 = _layer_norm(ALPHA * x_ref[...] + y, g_ref[...], beta_ref[...])
    y_ref[...] = x2
    yb_ref[...] = x2.astype(BF16)

    xh, xl = _split_bf16(x2)
    logits = _dot(xh, wr_hi_ref[...]) + _dot(xl, wr_hi_ref[...]) + _dot(xh, wr_lo_ref[...])
    lane = lax.broadcasted_iota(jnp.int32, logits.shape, 1)
    logits = jnp.where(lane < N_EXPERTS, logits, -jnp.inf)
    v1 = jnp.max(logits, -1, keepdims=True)
    i1 = jnp.min(jnp.where(logits == v1, lane, LANES), -1, keepdims=True)
    rest_l = jnp.where(lane == i1, -jnp.inf, logits)
    v2 = jnp.max(rest_l, -1, keepdims=True)
    i2 = jnp.min(jnp.where(rest_l == v2, lane, LANES), -1, keepdims=True)
    e2 = jnp.exp(v2 - v1)
    g1 = 1.0 / (1.0 + e2)
    g2 = e2 / (1.0 + e2)
    route = jnp.where(lane == 0, i1.astype(F32), 0.0)
    route = jnp.where(lane == 1, i2.astype(F32), route)
    route = jnp.where(lane == 2, g1, route)
    route = jnp.where(lane == 3, g2, route)
    route_ref[...] = route


def _residue_major(a, d, tm):
    rows, width = a.shape
    a = a.reshape(rows // tm, tm // d, d, width)
    return jnp.transpose(a, (2, 0, 1, 3)).reshape(1, d, rows // d, width)


def _merge(x, groups, weights, tm, n_prompt_rows, tiles_per_batch):
    m, d_model = x.shape
    n_p = n_prompt_rows // tm

    def pair(prompt, sample):
        _, d, _, width = prompt.shape
        blk = (None, d, tm // d, width)

        def prompt_map(i):
            j = jnp.minimum(i, n_p - 1)
            return (j // tiles_per_batch, 0, j % tiles_per_batch, 0)

        return [pl.BlockSpec(blk, prompt_map), pl.BlockSpec(blk, lambda i: (0, 0, jnp.maximum(i - n_p, 0), 0))]

    row = lambda width: pl.BlockSpec((tm, width), lambda i: (i, 0))
    o_in = [a for g in groups for a in g[:2]]
    l_in = [a for g in groups for a in g[2:]]
    return pl.pallas_call(
        functools.partial(_merge_kernel, n_prompt_tiles=n_p),
        grid=(m // tm,),
        in_specs=[row(d_model)]
                 + [s for g in groups for s in pair(*g[:2])] + [s for g in groups for s in pair(*g[2:])]
                 + [_full_spec(a) for a in weights],
        out_specs=[row(d_model), row(d_model), row(LANES)],
        out_shape=[jax.ShapeDtypeStruct((m, d_model), F32), jax.ShapeDtypeStruct((m, d_model), BF16),
                   jax.ShapeDtypeStruct((m, LANES), F32)],
        scratch_shapes=[pltpu.VMEM((W_C // LANES, tm, LANES), F32), pltpu.VMEM((1, tm, LANES), F32)],
        compiler_params=_params(1),
        name="merge_out_ln_route",
    )(x, *o_in, *l_in, *weights)


def _moe_kernel(te_ref, nt_ref, x_ref, wg_ref, wu_ref, wd_ref, o_ref):
    t = pl.program_id(0)

    @pl.when(t < nt_ref[0])
    def _():
        o_ref[...] = _swiglu(x_ref[...], wg_ref, wu_ref, wd_ref).astype(o_ref.dtype)

    @pl.when(t >= nt_ref[0])
    def _():
        o_ref[...] = jnp.zeros(o_ref.shape, o_ref.dtype)


def _moe(tile_expert, n_tiles_used, xs, wg, wu, wd, tm):
    p, d = xs.shape
    f = wg.shape[2]
    grid_spec = pltpu.PrefetchScalarGridSpec(
        num_scalar_prefetch=2,
        grid=(p // tm,),
        in_specs=[pl.BlockSpec((tm, d), lambda t, te, nt: (t, 0)),
                  pl.BlockSpec((None, d, f), lambda t, te, nt: (te[t], 0, 0), pipeline_mode=pl.Buffered(1)),
                  pl.BlockSpec((None, d, f), lambda t, te, nt: (te[t], 0, 0), pipeline_mode=pl.Buffered(1)),
                  pl.BlockSpec((None, f, d), lambda t, te, nt: (te[t], 0, 0), pipeline_mode=pl.Buffered(1))],
        out_specs=pl.BlockSpec((tm, d), lambda t, te, nt: (t, 0)),
    )
    return pl.pallas_call(
        _moe_kernel,
        grid_spec=grid_spec,
        out_shape=jax.ShapeDtypeStruct((p, d), BF16),
        compiler_params=_params(1),
        name="moe_experts",
    )(tile_expert, n_tiles_used, xs, wg, wu, wd)


def _final_kernel(x_ref, ya_ref, yb_ref, route_ref, g_ref, beta_ref, op_ref, os_ref, *, n_prompt_tiles):
    route = route_ref[...]
    y = route[:, 2:3] * ya_ref[...].astype(F32) + route[:, 3:4] * yb_ref[...].astype(F32)
    out = _layer_norm(ALPHA * x_ref[...] + y, g_ref[...], beta_ref[...])
    i = pl.program_id(0)

    @pl.when(i < n_prompt_tiles)
    def _():
        op_ref[...] = out

    @pl.when(i >= n_prompt_tiles)
    def _():
        os_ref[...] = out


def _final(x, ya, yb, route, g, beta, tm, n_prompt_rows):
    m, d = x.shape
    n_p = n_prompt_rows // tm
    row = lambda arr: pl.BlockSpec((tm, arr.shape[1]), lambda i: (i, 0))
    return pl.pallas_call(
        functools.partial(_final_kernel, n_prompt_tiles=n_p),
        grid=(m // tm,),
        in_specs=[row(x), row(ya), row(yb), row(route), _full_spec(g), _full_spec(beta)],
        out_specs=[pl.BlockSpec((tm, d), lambda i: (jnp.minimum(i, n_p - 1), 0)),
                   pl.BlockSpec((tm, d), lambda i: (jnp.maximum(i - n_p, 0), 0))],
        out_shape=[jax.ShapeDtypeStruct((n_prompt_rows, d), F32),
                   jax.ShapeDtypeStruct((m - n_prompt_rows, d), F32)],
        compiler_params=_params(1),
        name="moe_combine_ln",
    )(x, ya, yb, route, g, beta)


def _rope_table(pos, scale):
    half = DR_A // 2
    inv = ROPE_BASE ** (-jnp.arange(half, dtype=F32) / half)
    ang = pos.astype(F32)[:, None] * inv[None, :]
    cos, sin = jnp.cos(ang), jnp.sin(ang)
    n = pos.shape[0]
    z = lambda w: jnp.zeros((n, w), F32)
    one = jnp.ones((n, DN_A), F32)
    cq = jnp.concatenate([one, cos, cos, z(32)], 1) * scale
    s1q = jnp.concatenate([z(DN_A), -sin, z(16), z(32)], 1) * scale
    s2q = jnp.concatenate([z(DN_A), z(16), sin, z(32)], 1) * scale
    ck = jnp.concatenate([cos, cos, z(96)], 1)
    s1k = jnp.concatenate([-sin, z(112)], 1)
    s2k = jnp.concatenate([z(16), sin, z(96)], 1)
    return jnp.concatenate([cq, s1q, s2q, ck, s1k, s2k], 1)


def _t5_bucket(dist):
    max_exact = N_BUCKETS // 2
    log_ratio = jnp.log(jnp.maximum(dist, 1).astype(F32) / max_exact) / math.log(MAX_DISTANCE / max_exact)
    large = jnp.minimum(max_exact + (log_ratio * (N_BUCKETS - max_exact)).astype(jnp.int32), N_BUCKETS - 1)
    return jnp.where(dist < max_exact, dist, large)


def _group_bias(rel_bias, g):
    d = C_DILATIONS[g]
    dist = d * jnp.arange(C_WINDOWS[g] // d + 1, dtype=jnp.int32)
    onehot = (_t5_bucket(dist)[:, None] == jnp.arange(N_BUCKETS)[None, :]).astype(F32)
    table = rel_bias[:, g * H_C:(g + 1) * H_C]
    return jnp.transpose(jnp.dot(onehot, table, precision=lax.Precision.HIGHEST))


def _prompt_bias(bias):
    n_back = bias.shape[1] - 1
    a = np.arange(C_BLOCK)[:, None]
    c = np.arange(2 * C_BLOCK)[None, :]
    j = a + C_BLOCK - c
    band = (j >= 0) & (j <= n_back)
    full = _select_columns(bias, np.clip(j, 0, n_back).reshape(-1))
    return jnp.where(band[None], full.reshape(-1, C_BLOCK, 2 * C_BLOCK), NEG).reshape(-1, 2 * C_BLOCK)


def _select_columns(bias, cols):
    sel = (jnp.arange(bias.shape[1], dtype=jnp.int32)[:, None] == jnp.asarray(cols, jnp.int32)[None, :])
    return jnp.dot(bias.astype(F32), sel.astype(F32), precision=lax.Precision.HIGHEST)


def _sample_masks(bias, g, n_buf, t_dec):
    d = C_DILATIONS[g]
    n_back = bias.shape[1] - 1
    t = np.arange(t_dec)[:, None]

    def build(key_pos):
        diff = n_buf + t - key_pos[None, :]
        ok = (diff >= 0) & (diff % d == 0) & (diff // d <= n_back)
        full = _select_columns(bias, np.clip(diff // d, 0, n_back).reshape(-1))
        mt = jnp.where(ok[None], full.reshape(H_C, t_dec, -1), NEG)
        return mt.reshape(H_C * t_dec, -1)

    mask_new = build(n_buf + np.arange(LANES))
    col_ok = (np.arange(LANES) < t_dec)[None, :]
    return build(np.arange(n_buf)), jnp.where(col_ok, mask_new, NEG)


def _prep_even(e_w_in, e_q_norm_g, e_kv_norm_g, e_w_uq, e_w_uk, e_w_uv, e_v_ln_g, e_v_ln_b, e_w_s, e_b_s, t_dec):
    o_kr = D_CQ + D_C
    o_uv = o_kr + DR_A
    d_model = e_w_in.shape[0]
    w_in = jnp.concatenate([e_w_in[:, :o_kr], e_w_in[:, o_uv:], e_w_in[:, o_kr:o_uv],
                            jnp.zeros((d_model, LANES - DR_A), F32)], axis=1).astype(BF16)
    w_uq = jnp.pad(e_w_uq, ((0, 0), (0, 0), (0, LANES - DN_A - DR_A))).reshape(D_CQ, H_A * LANES).astype(BF16)
    w_uk_pad = jnp.pad(e_w_uk, ((0, 0), (0, 0), (0, LANES - DN_A))).reshape(D_C, H_A * LANES)
    w_uv = e_w_uv.reshape(D_C, H_A * DV_A)
    place = np.zeros((LANES, H_A, LANES), np.float32)
    for i in range(DR_A):
        place[i, :, DN_A + i] = 1.0
    place = place.reshape(LANES, H_A * LANES)
    w_kv = jnp.concatenate([
        jnp.concatenate([w_uk_pad, w_uv], axis=1),
        jnp.concatenate([jnp.asarray(place), jnp.zeros((LANES, H_A * DV_A), F32)], axis=1)], axis=0).astype(BF16)

    tril = np.tril(np.ones((CHUNK, CHUNK), bool))
    wg_p = jnp.where(tril[None], e_w_s, 0.0)
    small = jnp.where(np.tril(np.ones((t_dec, t_dec), bool))[None], e_w_s[:, :t_dec, :t_dec], 0.0)
    eye = jnp.eye(CHUNK // t_dec, dtype=F32)
    wg_s = jnp.einsum("ab,gts->gatbs", eye, small).reshape(G_B, CHUNK, CHUNK)
    wgate = jnp.stack([wg_p, wg_s]).astype(BF16)
    bg_p = jnp.broadcast_to(e_b_s[:, :, None], (G_B, CHUNK, LANES))
    bg_s = jnp.broadcast_to(jnp.tile(e_b_s[:, :t_dec], (1, CHUNK // t_dec))[:, :, None], (G_B, CHUNK, LANES))
    bgate = jnp.stack([bg_p, bg_s]).astype(F32)

    sel = np.zeros((LANES, Q_LAT_W), np.float32)
    for i in range(DR_A):
        sel[DN_A + i, D_C + i] = 1.0
    wq2 = jnp.broadcast_to(jnp.asarray(sel)[:, None, :], (LANES, H_A, Q_LAT_W))
    uk_t = jnp.transpose(e_w_uk, (2, 1, 0))
    wq2 = wq2.at[:DN_A, :, :D_C].add(uk_t)
    wq2 = wq2.reshape(LANES, H_A * Q_LAT_W).astype(BF16)
    return dict(w_in=w_in, gq=e_q_norm_g.reshape(1, -1), gkv=e_kv_norm_g.reshape(1, -1), w_uq=w_uq, w_kv=w_kv,
                vg=e_v_ln_g.reshape(1, -1), vb=e_v_ln_b.reshape(1, -1), wgate=wgate, bgate=bgate,
                wq2=wq2, wuv=w_uv.astype(BF16))


def _pick_tile(m_prompt, m_sample, cap):
    t = cap
    while m_prompt % t or m_sample % t:
        t //= 2
    assert t >= CHUNK
    return t


def kernel(x_prompt, x_sample, cache_mla_ckv, cache_mla_kpe, cache_dsw_kv0, cache_dsw_kv1, cache_dsw_kv2, page_table, ln_g, ln_b, e_w_in, e_q_norm_g, e_kv_norm_g, e_w_uq, e_w_uk, e_w_uv, e_v_ln_g, e_v_ln_b, e_w_s, e_b_s, e_w_out, ffn_w_gate, ffn_w_up, ffn_w_down, o_w_in, o_w_out, rel_bias, moe_w_router, moe_w_gate, moe_w_up, moe_w_down):
    nb_p, seq, d_model = x_prompt.shape
    nb_s, t_dec, _ = x_sample.shape
    past_len = page_table.shape[1] * PAGE_SIZE
    mp, ms = nb_p * seq, nb_s * t_dec
    m = mp + ms
    tm = _pick_tile(seq, ms, 512)
    tm_p = _pick_tile(seq, seq, 512)
    tm_s = _pick_tile(ms, ms, 512)
    assert CHUNK % t_dec == 0 and seq % CHUNK == 0 and mp % tm_s == 0

    xp0, xs0 = x_prompt.reshape(mp, d_model), x_sample.reshape(ms, d_model)
    ln = lambda l, j: (ln_g[l, j].reshape(1, -1), ln_b[l, j].reshape(1, -1))

    prep = _prep_even(e_w_in[0], e_q_norm_g[0], e_kv_norm_g[0], e_w_uq[0], e_w_uk[0], e_w_uv[0], e_v_ln_g[0],
                      e_v_ln_b[0], e_w_s[0], e_b_s[0], t_dec)
    scale = (DN_A + DR_A) ** -0.5
    pos = jnp.concatenate([jnp.arange(seq, dtype=jnp.int32),
                           past_len + (jnp.arange(tm, dtype=jnp.int32) % t_dec)])
    tab = _rope_table(pos, scale)
    q, k, v, gated, ckv_p, kpe_p, ckv_s, kpe_s, vln_s = _even_in(xp0, xs0, prep, tab, tm, seq // tm)

    att_p = _mla_prompt(q, k, v, nb_p, seq, min(256, seq))
    q_s = jnp.transpose(q[:, mp:].astype(F32), (1, 0, 2)).reshape(nb_s, t_dec, H_A * LANES)
    ckv_s = ckv_s.reshape(nb_s, t_dec, D_C)
    kpe_s = kpe_s.reshape(nb_s, t_dec, LANES)
    att_s = _mla_sample(page_table, q_s, ckv_s, kpe_s, prep["wq2"], prep["wuv"], cache_mla_ckv,
                        jnp.swapaxes(cache_mla_kpe, 2, 3))
    att_s = jnp.transpose(att_s.reshape(ms, H_A // 2, LANES), (1, 0, 2)).astype(BF16)

    w_out_e = e_w_out[0].astype(BF16)
    n_att = H_A * DV_A
    x1, x1b = _layer0_out(xp0, xs0, att_p, att_s, gated,
                          [w_out_e[:n_att], w_out_e[n_att:], *ln(0, 0), _to_bf16(ffn_w_gate[0]),
                           _to_bf16(ffn_w_up[0]), _to_bf16(ffn_w_down[0]), *ln(0, 1)], tm)

    ng = N_GROUPS_C
    w_in_o = _to_bf16(o_w_in[0]).reshape(d_model, ng, 3 * W_C)
    caches = (cache_dsw_kv0, cache_dsw_kv1, cache_dsw_kv2)
    groups, dsw_p, dsw_s = [], [], []
    for g in range(ng):
        w_g = w_in_o[:, g]
        bias = _group_bias(rel_bias, g)
        qd, kd, vd, kv_tail = _odd_in_prompt(x1b, w_g, g, nb_p, seq, tm_p)
        win = min(C_WINDOWS[g], seq)
        dsw_p.append(jnp.transpose(kv_tail.reshape(nb_p, 2, H_C, DH_C, win), (0, 4, 1, 2, 3))[None])
        o_p, lse_p = _dsw_prompt(qd, kd, vd, _prompt_bias(bias), g)

        q_new, kv_new = _odd_in_sample(x1b, w_g, g, mp, ms, tm_s)
        dsw_s.append(kv_new.reshape(1, nb_s, t_dec, 2, H_C, DH_C))
        n_buf = caches[g].shape[2]
        buf_t = jnp.transpose(caches[g][0], (0, 2, 3, 4, 1))
        q_hm = jnp.transpose(q_new.reshape(nb_s, t_dec, H_C, DH_C), (0, 2, 1, 3))
        mask, mask_new = _sample_masks(bias, g, n_buf, t_dec)
        o_s, lse_s = _dsw_sample(q_new, q_hm, kv_new, buf_t, mask, mask_new)
        d = C_DILATIONS[g]
        groups.append((o_p, _residue_major(o_s.astype(BF16), d, tm), lse_p, _residue_major(lse_s, d, tm)))

    expand = np.zeros((LANES, W_C), np.float32)
    for h in range(H_C):
        expand[h, h * DH_C:(h + 1) * DH_C] = 1.0
    wr = jnp.pad(moe_w_router[0], ((0, 0), (0, LANES - N_EXPERTS)))
    wr_hi, wr_lo = _split_bf16(wr)
    weights = [jnp.asarray(expand, BF16), o_w_out[0].astype(BF16), *ln(1, 0), wr_hi, wr_lo]
    x2, x2b, route = _merge(x1, groups, weights, tm, mp, seq // tm)

    tmoe = 512
    idx = route[:, :TOP_K].astype(jnp.int32)
    flat_e = idx.reshape(-1)
    onehot = (flat_e[:, None] == jnp.arange(N_EXPERTS)[None, :]).astype(jnp.int32)
    csum = jnp.cumsum(onehot, axis=0)
    counts = csum[-1]
    rank = jnp.sum(csum * onehot, axis=1) - 1
    padded = ((counts + tmoe - 1) // tmoe) * tmoe
    ends = jnp.cumsum(padded)
    starts = ends - padded
    pos_sorted = jnp.sum(starts[None, :] * onehot, axis=1) + rank
    p_rows = ((TOP_K * m + tmoe - 1) // tmoe + N_EXPERTS) * tmoe
    n_tiles = p_rows // tmoe
    tile_start = jnp.arange(n_tiles, dtype=jnp.int32) * tmoe
    tile_expert = jnp.minimum(jnp.sum(tile_start[:, None] >= ends[None, :], axis=1), N_EXPERTS - 1).astype(jnp.int32)
    n_used = (ends[-1] // tmoe).astype(jnp.int32).reshape(1)
    order = jnp.argsort(flat_e, stable=True).astype(jnp.int32)
    shift = starts - (jnp.cumsum(counts) - counts)
    row_shift = jnp.repeat(jnp.sum(shift[None, :] * (tile_expert[:, None] == jnp.arange(N_EXPERTS)[None, :]), axis=1),
                           tmoe)
    q_idx = jnp.clip(jnp.arange(p_rows, dtype=jnp.int32) - row_shift, 0, TOP_K * m - 1)
    src = order.at[q_idx].get(mode="promise_in_bounds") // TOP_K
    xs = x2b.at[src].get(mode="promise_in_bounds")
    ys = _moe(tile_expert, n_used, xs, _to_bf16(moe_w_gate[0]), _to_bf16(moe_w_up[0]), _to_bf16(moe_w_down[0]),
              tmoe)
    pos2 = pos_sorted.reshape(m, TOP_K)
    ya = ys.at[pos2[:, 0]].get(mode="promise_in_bounds")
    yb = ys.at[pos2[:, 1]].get(mode="promise_in_bounds")
    y_p, y_s = _final(x2, ya, yb, route, *ln(1, 1), tm, mp)

    y_prompt = y_p.reshape(nb_p, seq, d_model)
    y_sample = y_s.reshape(nb_s, t_dec, d_model)
    n_pg = seq // PAGE_SIZE
    new_ckv_prompt = ckv_p.reshape(1, nb_p, n_pg, PAGE_SIZE, D_C)
    new_kpe_prompt = kpe_p[:, :DR_A].reshape(1, nb_p, n_pg, PAGE_SIZE, DR_A)
    new_ckv_sample = ckv_s.reshape(1, nb_s, t_dec, D_C)
    new_kpe_sample = kpe_s[:, :, :DR_A].reshape(1, nb_s, t_dec, DR_A)
    new_gate_v_sample = vln_s.reshape(1, nb_s, t_dec, D_B)
    return (y_prompt, y_sample, new_ckv_prompt, new_kpe_prompt, new_ckv_sample, new_kpe_sample, new_gate_v_sample,
            dsw_p[0], dsw_p[1], dsw_p[2], dsw_s[0], dsw_s[1], dsw_s[2])
```

```python
import functools
import math

import numpy as np
import jax
import jax.numpy as jnp
from jax import lax
from jax.experimental import pallas as pl
from jax.experimental.pallas import tpu as pltpu

BF16 = jnp.bfloat16
F32 = jnp.float32

H_A, DN_A, DR_A, DV_A = 8, 64, 32, 64
D_CQ, D_C = 384, 256
ROPE_BASE = 10000.0
CHUNK, G_B, D_B = 128, 4, 512
C_WINDOWS = (128, 512, 2048)
C_DILATIONS = (1, 4, 16)
N_GROUPS_C = 3
H_C, DH_C = 16, 64
W_C = H_C * DH_C
C_BLOCK = 128
N_BUCKETS, MAX_DISTANCE = 32, 2048
N_EXPERTS, TOP_K = 8, 2
DEPTH = 2
ALPHA = (2 * DEPTH) ** 0.25
LN_EPS = 1e-5
RMS_EPS = 1e-6
PAGE_SIZE = 128
NEG = -1e30

LANES = 128
MXU_DIM = 256
V7X_VMEM_BYTES = 64 * 2**20
VMEM_LIMIT = 56 * 2**20


def _dot(a, b):
    return jnp.dot(a, b, preferred_element_type=F32)


def _dot_nt(a, b):
    return lax.dot_general(a, b, (((1,), (1,)), ((), ())), preferred_element_type=F32)


def _params(n_axes, vmem=VMEM_LIMIT):
    return pltpu.CompilerParams(dimension_semantics=("arbitrary",) * n_axes, vmem_limit_bytes=vmem)


def _layer_norm(x, g, b):
    mu = jnp.mean(x, -1, keepdims=True)
    xc = x - mu
    var = jnp.mean(xc * xc, -1, keepdims=True)
    return xc * lax.rsqrt(var + LN_EPS) * g + b


def _rms_norm(x, g):
    return x * lax.rsqrt(jnp.mean(x * x, -1, keepdims=True) + RMS_EPS) * g


def _full_spec(a):
    return pl.BlockSpec(a.shape, lambda *_: (0,) * a.ndim)


def _rope_apply(x, c, s1, s2):
    return x * c + pltpu.roll(x, LANES - 16, 1) * s1 + pltpu.roll(x, 16, 1) * s2


def _even_in_kernel(xp_ref, xs_ref, w_in_ref, gq_ref, gkv_ref, w_uq_ref, w_kv_ref, vg_ref, vb_ref, tab_ref,
                    wgate_ref, bgate_ref,
                    q_ref, k_ref, v_ref, gated_ref, ckv_ref, kpe_ref, vln_ref, *, n_prompt_tiles):
    tm = xp_ref.shape[0]
    x = jnp.where(pl.program_id(0) < n_prompt_tiles, xp_ref[...], xs_ref[...])
    h = _dot(x.astype(BF16), w_in_ref[...])
    c_q = h[:, :D_CQ]
    c_kv = h[:, D_CQ:D_CQ + D_C]
    o_u = D_CQ + D_C
    u = jax.nn.gelu(h[:, o_u:o_u + D_B])
    vv = jax.nn.gelu(h[:, o_u + D_B:o_u + 2 * D_B])
    kr = h[:, o_u + 2 * D_B:]

    tab = tab_ref[...]
    cq, s1q, s2q = tab[:, 0:128], tab[:, 128:256], tab[:, 256:384]
    ck, s1k, s2k = tab[:, 384:512], tab[:, 512:640], tab[:, 640:768]

    q = _dot(_rms_norm(c_q, gq_ref[...]).astype(BF16), w_uq_ref[...])
    for hb in range(H_A):
        q_ref[hb] = _rope_apply(q[:, hb * LANES:(hb + 1) * LANES], cq, s1q, s2q).astype(BF16)

    ckv = _rms_norm(c_kv, gkv_ref[...])
    kpe = _rope_apply(kr, ck, s1k, s2k)
    vln = _layer_norm(vv, vg_ref[...], vb_ref[...])
    ckv_ref[...] = ckv
    kpe_ref[...] = kpe
    vln_ref[...] = vln
    kv = _dot(jnp.concatenate([ckv.astype(BF16), kpe.astype(BF16)], axis=1), w_kv_ref[...])
    for hb in range(H_A):
        k_ref[hb] = kv[:, hb * LANES:(hb + 1) * LANES].astype(BF16)
    for p in range(H_A // 2):
        v_ref[p] = kv[:, (H_A + p) * LANES:(H_A + p + 1) * LANES].astype(BF16)

    vb16 = vln.astype(BF16)
    gw = D_B // G_B
    for c in range(tm // CHUNK):
        rs = slice(c * CHUNK, (c + 1) * CHUNK)
        for g in range(G_B):
            cs = slice(g * gw, (g + 1) * gw)
            f = _dot(wgate_ref[g], vb16[rs, cs]) + bgate_ref[g]
            gated_ref[rs, cs] = (u[rs, cs] * f).astype(BF16)


def _even_in(xp, xs, prep, tab, tm, tab_blocks):
    (mp, d_model), ms = xp.shape, xs.shape[0]
    m = mp + ms
    n_prompt_tiles = mp // tm

    def tab_map(i):
        return (jnp.where(i < n_prompt_tiles, i % tab_blocks, tab_blocks), 0)

    def gate_map(i):
        return (jnp.where(i < n_prompt_tiles, 0, 1), 0, 0, 0)

    row = lambda w: pl.BlockSpec((tm, w), lambda i: (i, 0))
    heads = lambda n: pl.BlockSpec((n, tm, LANES), lambda i: (0, i, 0))
    outs = [
        jax.ShapeDtypeStruct((H_A, m, LANES), BF16),
        jax.ShapeDtypeStruct((H_A, m, LANES), BF16),
        jax.ShapeDtypeStruct((H_A // 2, m, LANES), BF16),
        jax.ShapeDtypeStruct((m, D_B), BF16),
        jax.ShapeDtypeStruct((m, D_C), F32),
        jax.ShapeDtypeStruct((m, LANES), F32),
        jax.ShapeDtypeStruct((m, D_B), F32),
    ]
    weights = [prep[n] for n in ("w_in", "gq", "gkv", "w_uq", "w_kv", "vg", "vb")]
    return pl.pallas_call(
        functools.partial(_even_in_kernel, n_prompt_tiles=n_prompt_tiles),
        grid=(m // tm,),
        in_specs=_two_group_specs(tm, d_model, n_prompt_tiles) + [_full_spec(a) for a in weights]
                 + [pl.BlockSpec((tm, 6 * LANES), tab_map),
                    pl.BlockSpec((None, G_B, CHUNK, CHUNK), gate_map),
                    pl.BlockSpec((None, G_B, CHUNK, LANES), gate_map)],
        out_specs=[heads(H_A), heads(H_A), heads(H_A // 2), row(D_B), row(D_C), row(LANES), row(D_B)],
        out_shape=outs,
        compiler_params=_params(1),
        name="even_in",
    )(xp, xs, *weights, tab, prep["wgate"], prep["bgate"])


def _mla_prompt_kernel(q_ref, k_ref, v_ref, o_ref, *, tq):
    seq = q_ref.shape[0]
    h = pl.program_id(1)
    lane = lax.broadcasted_iota(jnp.int32, (tq, LANES), 1)
    row = lax.broadcasted_iota(jnp.int32, (tq, tq), 0)
    col = lax.broadcasted_iota(jnp.int32, (tq, tq), 1)
    diag_mask = col <= row
    for i in range(seq // tq):
        cur = slice(i * tq, (i + 1) * tq)
        past = slice(0, i * tq)
        qi = q_ref[cur, :]
        s_d = jnp.where(diag_mask, _dot_nt(qi, k_ref[cur, :]), NEG)
        m = jnp.max(s_d, -1, keepdims=True)
        if i:
            s_p = _dot_nt(qi, k_ref[past, :])
            m = jnp.maximum(m, jnp.max(s_p, -1, keepdims=True))
        p_d = jnp.exp(s_d - m)
        l = jnp.sum(p_d, -1, keepdims=True)
        acc = _dot(p_d.astype(BF16), v_ref[cur, :])
        if i:
            p_p = jnp.exp(s_p - m)
            l = l + jnp.sum(p_p, -1, keepdims=True)
            acc = acc + _dot(p_p.astype(BF16), v_ref[past, :])
        o = (acc / l).astype(BF16)

        @pl.when(h % 2 == 0)
        def _():
            o_ref[cur, :] = o

        @pl.when(h % 2 == 1)
        def _():
            o_ref[cur, :] = jnp.where(lane < DV_A, o_ref[cur, :], o)


def _mla_prompt(q, k, v, n_batch, seq, tq):
    per_head = lambda div: pl.BlockSpec((None, seq, LANES), lambda b, h: (h // div, b, 0))
    return pl.pallas_call(
        functools.partial(_mla_prompt_kernel, tq=tq),
        grid=(n_batch, H_A),
        in_specs=[per_head(1), per_head(1), per_head(2)],
        out_specs=per_head(2),
        out_shape=jax.ShapeDtypeStruct((H_A // 2, n_batch * seq, LANES), BF16),
        compiler_params=_params(2),
        name="mla_prompt",
    )(q, k, v)


Q_LAT_W = D_C + LANES
PAGE_LOOP_UNROLL = 8


def _mla_sample_kernel(pt_ref, q_ref, ckvn_ref, kpen_ref, wq_ref, wuv_ref, ckv_hbm, kpe_hbm, o_ref,
                       ckv_buf, kpe_buf, sem, *, n_pages, t_dec):
    b = pl.program_id(0)
    rows = H_A * t_dec

    def page_copies(seq, slot, p):
        page = pt_ref[seq * n_pages + p]
        return (pltpu.make_async_copy(ckv_hbm.at[0, page], ckv_buf.at[slot, p], sem.at[0, slot]),
                pltpu.make_async_copy(kpe_hbm.at[0, page], kpe_buf.at[slot, p], sem.at[1, slot]))

    def fetch(seq, slot):
        def body(p, carry):
            for cp in page_copies(seq, slot, p):
                cp.start()
            return carry
        lax.fori_loop(0, n_pages, body, 0, unroll=PAGE_LOOP_UNROLL)

    @pl.when(b == 0)
    def _():
        fetch(0, 0)

    @pl.when(b + 1 < pl.num_programs(0))
    def _():
        fetch(b + 1, (b + 1) % 2)

    slot = b % 2

    def wait_body(p, carry):
        for cp in page_copies(b, slot, p):
            cp.wait()
        return carry
    lax.fori_loop(0, n_pages, wait_body, 0, unroll=PAGE_LOOP_UNROLL)

    q = q_ref[...]
    qs = jnp.concatenate([q[:, h * LANES:(h + 1) * LANES] for h in range(H_A)], axis=0).astype(BF16)
    full = _dot(qs, wq_ref[...])
    ql = jnp.concatenate(
        [full[h * t_dec:(h + 1) * t_dec, h * Q_LAT_W:(h + 1) * Q_LAT_W] for h in range(H_A)], axis=0)
    qlat = ql[:, :D_C].astype(BF16)
    qrope = ql[:, D_C:D_C + DR_A].astype(BF16)

    def partial_softmax(s, vals):
        m = jnp.max(s, -1, keepdims=True)
        p = jnp.exp(s - m)
        return m, jnp.sum(p, -1, keepdims=True), _dot(p.astype(BF16), vals)

    ck = ckv_buf[slot].reshape(n_pages * PAGE_SIZE, D_C).astype(BF16)
    kt = jnp.concatenate([kpe_buf[slot, p].astype(BF16) for p in range(n_pages)], axis=1)
    stats = [partial_softmax(_dot_nt(qlat, ck) + _dot(qrope, kt), ck)]

    pad = PAGE_SIZE - t_dec
    cn = jnp.concatenate([ckvn_ref[...], jnp.zeros((pad, D_C), F32)], axis=0).astype(BF16)
    kn = jnp.concatenate([kpen_ref[...][:, :DR_A], jnp.zeros((pad, DR_A), F32)], axis=0).astype(BF16)
    r = lax.broadcasted_iota(jnp.int32, (rows, PAGE_SIZE), 0) % t_dec
    cc = lax.broadcasted_iota(jnp.int32, (rows, PAGE_SIZE), 1)
    stats.append(partial_softmax(jnp.where(cc <= r, _dot_nt(qlat, cn) + _dot_nt(qrope, kn), NEG), cn))

    m = functools.reduce(jnp.maximum, [st[0] for st in stats])
    scales = [jnp.exp(st[0] - m) for st in stats]
    l = sum(sc * st[1] for sc, st in zip(scales, stats))
    acc = sum(sc * st[2] for sc, st in zip(scales, stats))
    o_lat = (acc / l).astype(BF16)
    full = _dot(o_lat, wuv_ref[...])
    rh = lax.broadcasted_iota(jnp.int32, full.shape, 0) // t_dec
    ch = lax.broadcasted_iota(jnp.int32, full.shape, 1) // DV_A
    full = jnp.where(rh == ch, full, 0.0)
    out = full[0:t_dec]
    for h in range(1, H_A):
        out = out + full[h * t_dec:(h + 1) * t_dec]
    o_ref[...] = out


def _mla_sample(page_table, q_s, ckv_new, kpe_new, wq2, wuv, cache_ckv, cache_kpe_t):
    nb, t_dec = q_s.shape[:2]
    n_pages = page_table.shape[1]

    per_b = lambda w: pl.BlockSpec((None, t_dec, w), lambda b, pt: (b, 0, 0))
    grid_spec = pltpu.PrefetchScalarGridSpec(
        num_scalar_prefetch=1,
        grid=(nb,),
        in_specs=[per_b(H_A * LANES), per_b(D_C), per_b(LANES),
                  pl.BlockSpec(wq2.shape, lambda b, pt: (0, 0)),
                  pl.BlockSpec(wuv.shape, lambda b, pt: (0, 0)),
                  pl.BlockSpec(memory_space=pl.ANY), pl.BlockSpec(memory_space=pl.ANY)],
        out_specs=per_b(H_A * DV_A),
        scratch_shapes=[pltpu.VMEM((2, n_pages, PAGE_SIZE, D_C), F32),
                        pltpu.VMEM((2, n_pages, DR_A, PAGE_SIZE), F32),
                        pltpu.SemaphoreType.DMA((2, 2))],
    )
    return pl.pallas_call(
        functools.partial(_mla_sample_kernel, n_pages=n_pages, t_dec=t_dec),
        grid_spec=grid_spec,
        out_shape=jax.ShapeDtypeStruct((nb, t_dec, H_A * DV_A), F32),
        compiler_params=_params(1),
        name="mla_sample",
    )(page_table.reshape(-1), q_s, ckv_new, kpe_new, wq2, wuv, cache_ckv, cache_kpe_t)


def _swiglu(xb, wg_ref, wu_ref, wd_ref):
    f = wg_ref.shape[1]
    step = 3 * MXU_DIM
    y = None
    for s in range(0, f, step):
        cs = slice(s, min(s + step, f))
        h = jax.nn.silu(_dot(xb, wg_ref[:, cs])) * _dot(xb, wu_ref[:, cs])
        t = _dot(h.astype(BF16), wd_ref[cs, :])
        y = t if y is None else y + t
    return y


def _layer0_out_kernel(xp_ref, xs_ref, ap_ref, as_ref, b_ref, wa_ref, wb_ref, g0_ref, b0_ref, wg_ref, wu_ref,
                       wd_ref, g1_ref, b1_ref, o_ref, ob_ref, *, n_prompt_tiles):
    is_prompt = pl.program_id(0) < n_prompt_tiles
    x = jnp.where(is_prompt, xp_ref[...], xs_ref[...])
    att = jnp.concatenate([jnp.where(is_prompt, ap_ref[p], as_ref[p]) for p in range(ap_ref.shape[0])], axis=1)
    mix = _dot(att, wa_ref[...]) + _dot(b_ref[...], wb_ref[...])
    x1 = _layer_norm(ALPHA * x + mix, g0_ref[...], b0_ref[...])
    y = _swiglu(x1.astype(BF16), wg_ref, wu_ref, wd_ref)
    out = _layer_norm(ALPHA * x1 + y, g1_ref[...], b1_ref[...])
    o_ref[...] = out
    ob_ref[...] = out.astype(BF16)


def _two_group_specs(tm, width, n_prompt_tiles):
    return [pl.BlockSpec((tm, width), lambda i: (jnp.minimum(i, n_prompt_tiles - 1), 0)),
            pl.BlockSpec((tm, width), lambda i: (jnp.maximum(i - n_prompt_tiles, 0), 0))]


def _layer0_out(xp, xs, a_p, a_s, b, weights, tm):
    (mp, d), ms = xp.shape, xs.shape[0]
    m = mp + ms
    n_p = mp // tm
    resident = lambda arr: pl.BlockSpec(arr.shape, lambda i: (0, 0), pipeline_mode=pl.Buffered(1))
    row = lambda width: pl.BlockSpec((tm, width), lambda i: (i, 0))
    pairs = a_p.shape[0]
    return pl.pallas_call(
        functools.partial(_layer0_out_kernel, n_prompt_tiles=n_p),
        grid=(m // tm,),
        in_specs=_two_group_specs(tm, d, n_p)
                 + [pl.BlockSpec((pairs, tm, LANES), lambda i: (0, jnp.minimum(i, n_p - 1), 0)),
                    pl.BlockSpec((pairs, tm, LANES), lambda i: (0, jnp.maximum(i - n_p, 0), 0)),
                    row(b.shape[1])]
                 + [resident(w) for w in weights],
        out_specs=[row(d), row(d)],
        out_shape=[jax.ShapeDtypeStruct((m, d), F32), jax.ShapeDtypeStruct((m, d), BF16)],
        compiler_params=_params(1),
        name="mix_out_ffn_ln",
    )(xp, xs, a_p, a_s, b, *weights)


def _odd_in_prompt_kernel(x_ref, w_ref, q_ref, k_ref, v_ref, kv_ref, scr_ref, *, d, scale, kv_rows):
    tm = x_ref.shape[0]
    n = tm // d
    x = x_ref[...]
    for part, (o_ref, s) in enumerate(((q_ref, scale), (k_ref, 1.0), (v_ref, 1.0))):
        y = _dot(x, w_ref[:, part * W_C:(part + 1) * W_C])
        if part:
            kv_ref[(part - 1) * W_C:part * W_C, :] = y[tm - kv_rows:, :].T
        if d == 1:
            o_ref[0] = (y * s).astype(BF16)
        else:
            for c in range(W_C // LANES):
                scr_ref[c] = y[:, c * LANES:(c + 1) * LANES] * s
            for r in range(d):
                o_ref[r] = jnp.concatenate(
                    [scr_ref[c, pl.ds(r, n, stride=d), :] for c in range(W_C // LANES)], axis=1).astype(BF16)


def _odd_in_prompt(xb, w_g, g, n_batch, seq, tm):
    d = C_DILATIONS[g]
    win = min(C_WINDOWS[g], seq)
    tiles = seq // tm
    assert tm % d == 0 and (tm // d) % 16 == 0
    if win >= tm:
        assert win % tm == 0
        kv_rows, first = tm, (seq - win) // tm
    else:
        assert tm % win == 0
        kv_rows, first = win, tiles - 1
    dil = pl.BlockSpec((None, d, tm // d, W_C), lambda i: (i // tiles, 0, i % tiles, 0))
    kv_spec = pl.BlockSpec((None, 2 * W_C, kv_rows),
                           lambda i: (i // tiles, 0, jnp.maximum(i % tiles - first, 0)))
    dil_shape = jax.ShapeDtypeStruct((n_batch, d, seq // d, W_C), BF16)
    return pl.pallas_call(
        functools.partial(_odd_in_prompt_kernel, d=d, scale=DH_C ** -0.5, kv_rows=kv_rows),
        grid=(n_batch * tiles,),
        in_specs=[pl.BlockSpec((tm, xb.shape[1]), lambda i: (i, 0)), _full_spec(w_g)],
        out_specs=[dil, dil, dil, kv_spec],
        out_shape=[dil_shape, dil_shape, dil_shape, jax.ShapeDtypeStruct((n_batch, 2 * W_C, win), F32)],
        scratch_shapes=[pltpu.VMEM((W_C // LANES, tm, LANES), F32)],
        compiler_params=_params(1),
        name=f"odd_in_prompt_g{g}",
    )(xb, w_g)


def _odd_in_sample_kernel(x_ref, w_ref, q_ref, kv_ref, *, scale):
    y = _dot(x_ref[...], w_ref[...])
    q_ref[...] = y[:, :W_C] * scale
    kv_ref[...] = y[:, W_C:]


def _odd_in_sample(xb, w_g, g, row0, n_rows, tm):
    return pl.pallas_call(
        functools.partial(_odd_in_sample_kernel, scale=DH_C ** -0.5),
        grid=(n_rows // tm,),
        in_specs=[pl.BlockSpec((tm, xb.shape[1]), lambda i: (row0 // tm + i, 0)), _full_spec(w_g)],
        out_specs=[pl.BlockSpec((tm, W_C), lambda i: (i, 0)), pl.BlockSpec((tm, 2 * W_C), lambda i: (i, 0))],
        out_shape=[jax.ShapeDtypeStruct((n_rows, W_C), F32), jax.ShapeDtypeStruct((n_rows, 2 * W_C), F32)],
        compiler_params=_params(1),
        name=f"odd_in_sample_g{g}",
    )(xb, w_g)


def _dsw_prompt_kernel(q_ref, k_ref, v_ref, bias_ref, o_ref, lse_ref, *, n_blk):
    blk = C_BLOCK
    lane = lax.broadcasted_iota(jnp.int32, (blk, LANES), 1)
    lo = lane < DH_C

    def blocks(items, first):
        row0 = lambda n: n * blk if isinstance(n, int) else pl.multiple_of(n * blk, blk)
        if first:
            sl = [(r, pl.ds(0, blk), pl.ds(0, blk)) for r, _ in items]
        else:
            sl = [(r, pl.ds(row0(n), blk), pl.ds(row0(n - 1), 2 * blk)) for r, n in items]
        scores = []
        for r, qs, ks in sl:
            parts = []
            for hp in range(H_C // 2):
                hs = slice(hp * LANES, (hp + 1) * LANES)
                qp = q_ref[r, qs, hs]
                zero = jnp.zeros_like(qp)
                qq = jnp.concatenate([jnp.where(lo, qp, zero), jnp.where(lo, zero, qp)], axis=0)
                parts.append(_dot_nt(qq, k_ref[r, ks, hs]))
            scores.append(jnp.concatenate(parts, axis=0) + (bias_ref[:, blk:] if first else bias_ref[...]))
        probs = []
        for s in scores:
            m = jnp.max(s, -1, keepdims=True)
            p = jnp.exp(s - m)
            l = jnp.sum(p, -1, keepdims=True)
            probs.append(((p * (1.0 / l)).astype(BF16), m + jnp.log(l)))
        for (r, qs, ks), (pn, lse) in zip(sl, probs):
            lse_tile = jnp.zeros((blk, LANES), F32)
            for hp in range(H_C // 2):
                hs = slice(hp * LANES, (hp + 1) * LANES)
                o2 = _dot(pn[2 * hp * blk:(2 * hp + 2) * blk], v_ref[r, ks, hs])
                o_ref[r, qs, hs] = jnp.where(lo, o2[:blk], o2[blk:]).astype(BF16)
                for h in (2 * hp, 2 * hp + 1):
                    lse_tile = jnp.where(lane == h, lse[h * blk:(h + 1) * blk], lse_tile)
            lse_ref[r, qs, :] = lse_tile

    def residue(r, carry):
        blocks([(r, 0)], True)
        if n_blk > 1:
            def body(n, c):
                blocks([(r, n)], False)
                return c
            lax.fori_loop(1, n_blk, body, 0)
        return carry

    lax.fori_loop(0, q_ref.shape[0], residue, 0)


def _dsw_prompt(q, k, v, bias, g):
    n_batch, d, length, w = q.shape
    assert length % C_BLOCK == 0
    spec = lambda width: pl.BlockSpec((None, d, length, width), lambda b: (b, 0, 0, 0))
    return pl.pallas_call(
        functools.partial(_dsw_prompt_kernel, n_blk=length // C_BLOCK),
        grid=(n_batch,),
        in_specs=[spec(w), spec(w), spec(w), _full_spec(bias)],
        out_specs=[spec(w), spec(LANES)],
        out_shape=[jax.ShapeDtypeStruct((n_batch, d, length, w), BF16),
                   jax.ShapeDtypeStruct((n_batch, d, length, LANES), F32)],
        compiler_params=_params(1),
        name=f"dsw_prompt_g{g}",
    )(q, k, v, bias)


def _dsw_sample_kernel(q_ref, qh_ref, kvn_ref, buf_ref, mask_ref, maskn_ref, o_ref, lse_ref, *, t_dec, n_seq):
    for i in range(n_seq):
        rs = slice(i * t_dec, (i + 1) * t_dec)
        o, lse = _dsw_sample_one(q_ref[rs, :], qh_ref.at[i], kvn_ref[rs, :], buf_ref.at[i], mask_ref, maskn_ref,
                                 t_dec)
        o_ref[rs, :] = o
        lse_ref[rs, :] = lse


def _dsw_sample_one(q, qh_ref, kvn, buf_ref, mask_ref, maskn_ref, t_dec):
    rows = H_C * t_dec
    rh = lax.broadcasted_iota(jnp.int32, (rows, W_C), 0) // t_dec
    ch = lax.broadcasted_iota(jnp.int32, (rows, W_C), 1) // DH_C
    same = rh == ch
    lane = lax.broadcasted_iota(jnp.int32, (rows, LANES), 1)
    rl = lax.broadcasted_iota(jnp.int32, (rows, LANES), 0) // t_dec
    pad = LANES - t_dec

    qbd = jnp.where(same, jnp.concatenate([q] * H_C, axis=0), 0.0).astype(BF16)
    kn = jnp.concatenate([kvn[:, :W_C], jnp.zeros((pad, W_C), F32)], axis=0).astype(BF16)
    vn = jnp.concatenate([kvn[:, W_C:], jnp.zeros((pad, W_C), F32)], axis=0).astype(BF16)
    s_n = _dot_nt(qbd, kn) + maskn_ref[...]

    s_b = jnp.concatenate(
        [_dot(qh_ref[h].astype(BF16), buf_ref[0, h].astype(BF16)) for h in range(H_C)], axis=0) + mask_ref[...]

    m = jnp.maximum(jnp.max(s_b, -1, keepdims=True), jnp.max(s_n, -1, keepdims=True))
    p_b = jnp.exp(s_b - m)
    p_n = jnp.exp(s_n - m)
    l = jnp.sum(p_b, -1, keepdims=True) + jnp.sum(p_n, -1, keepdims=True)
    p_b = p_b / l
    o_n = jnp.where(same, _dot((p_n / l).astype(BF16), vn), 0.0)
    lse = jnp.where(lane == rl, m + jnp.log(l), 0.0)
    o_acc = o_n[0:t_dec]
    lse_acc = lse[0:t_dec]
    for h in range(1, H_C):
        o_acc = o_acc + o_n[h * t_dec:(h + 1) * t_dec]
        lse_acc = lse_acc + lse[h * t_dec:(h + 1) * t_dec]
    o_b = jnp.concatenate(
        [_dot_nt(p_b[h * t_dec:(h + 1) * t_dec].astype(BF16), buf_ref[1, h].astype(BF16)) for h in range(H_C)],
        axis=1)
    return o_acc + o_b, lse_acc


DSW_SAMPLE_BLOCK_BYTES = 8 * 2**20


def _dsw_sample(q_s, q_hm, kv_new, buf_t, mask, mask_new):
    nb, _, t_dec, _ = q_hm.shape
    ms = nb * t_dec
    n_seq = max(1, min(8, DSW_SAMPLE_BLOCK_BYTES // (4 * math.prod(buf_t.shape[1:]))))
    while nb % n_seq:
        n_seq -= 1
    row = lambda width: pl.BlockSpec((n_seq * t_dec, width), lambda b: (b, 0))
    return pl.pallas_call(
        functools.partial(_dsw_sample_kernel, t_dec=t_dec, n_seq=n_seq),
        grid=(nb // n_seq,),
        in_specs=[row(W_C), pl.BlockSpec((n_seq,) + q_hm.shape[1:], lambda b: (b, 0, 0, 0)), row(2 * W_C),
                  pl.BlockSpec((n_seq,) + buf_t.shape[1:], lambda b: (b, 0, 0, 0, 0)),
                  _full_spec(mask), _full_spec(mask_new)],
        out_specs=[row(W_C), row(LANES)],
        out_shape=[jax.ShapeDtypeStruct((ms, W_C), F32), jax.ShapeDtypeStruct((ms, LANES), F32)],
        compiler_params=_params(1),
        name="dsw_sample",
    )(q_s, q_hm, kv_new, buf_t, mask, mask_new)


def _split_bf16(x):
    hi = x.astype(BF16)
    return hi, (x - hi.astype(F32)).astype(BF16)


def _interleave(refs, is_prompt, scr_ref):
    p_ref, s_ref = refs
    d, n, width = p_ref.shape
    row = lambda r: jnp.where(is_prompt, p_ref[r], s_ref[r]).astype(F32)
    if d == 1:
        return row(0)
    slabs = width // LANES
    for r in range(d):
        val = row(r)
        for c in range(slabs):
            scr_ref[c, pl.ds(r, n, stride=d), :] = val[:, c * LANES:(c + 1) * LANES]
    return jnp.concatenate([scr_ref[c] for c in range(slabs)], axis=1)


def _merge_kernel(x_ref, *refs, n_prompt_tiles):
    ng = N_GROUPS_C
    o_refs = [refs[2 * g:2 * g + 2] for g in range(ng)]
    l_refs = [refs[2 * (ng + g):2 * (ng + g) + 2] for g in range(ng)]
    e_ref, w_ref, g_ref, beta_ref, wr_hi_ref, wr_lo_ref, y_ref, yb_ref, route_ref, o_scr, l_scr = refs[4 * ng:]
    is_prompt = pl.program_id(0) < n_prompt_tiles
    lses = [_interleave(l, is_prompt, l_scr) for l in l_refs]
    mx = jnp.maximum(jnp.maximum(lses[0], lses[1]), lses[2])
    es = [jnp.exp(l - mx) for l in lses]
    den = es[0] + es[1] + es[2]
    acc = None
    for e, o in zip(es, o_refs):
        hi, lo = _split_bf16(e / den)
        wexp = _dot(hi, e_ref[...]) + _dot(lo, e_ref[...])
        term = wexp * _interleave(o, is_prompt, o_scr)
        acc = term if acc is None else acc + term
    y = _dot(acc.astype(BF16), w_ref[...])
    x2 = _layer_norm(ALPHA * x_ref[...] + y, g_ref[...], beta_ref[...])
    y_ref[...] = x2
    yb_ref[...] = x2.astype(BF16)

    xh, xl = _split_bf16(x2)
    logits = _dot(xh, wr_hi_ref[...]) + _dot(xl, wr_hi_ref[...]) + _dot(xh, wr_lo_ref[...])
    lane = lax.broadcasted_iota(jnp.int32, logits.shape, 1)
    logits = jnp.where(lane < N_EXPERTS, logits, -jnp.inf)
    v1 = jnp.max(logits, -1, keepdims=True)
    i1 = jnp.min(jnp.where(logits == v1, lane, LANES), -1, keepdims=True)
    rest_l = jnp.where(lane == i1, -jnp.inf, logits)
    v2 = jnp.max(rest_l, -1, keepdims=True)
    i2 = jnp.min(jnp.where(rest_l == v2, lane, LANES), -1, keepdims=True)
    e2 = jnp.exp(v2 - v1)
    g1 = 1.0 / (1.0 + e2)
    g2 = e2 / (1.0 + e2)
    route = jnp.where(lane == 0, i1.astype(F32), 0.0)
    route = jnp.where(lane == 1, i2.astype(F32), route)
    route = jnp.where(lane == 2, g1, route)
    route = jnp.where(lane == 3, g2, route)
    route_ref[...] = route


def _residue_major(a, d, tm):
    rows, width = a.shape
    a = a.reshape(rows // tm, tm // d, d, width)
    return jnp.transpose(a, (2, 0, 1, 3)).reshape(1, d, rows // d, width)


def _merge(x, groups, weights, tm, n_prompt_rows, tiles_per_batch):
    m, d_model = x.shape
    n_p = n_prompt_rows // tm

    def pair(prompt, sample):
        _, d, _, width = prompt.shape
        blk = (None, d, tm // d, width)

        def prompt_map(i):
            j = jnp.minimum(i, n_p - 1)
            return (j // tiles_per_batch, 0, j % tiles_per_batch, 0)

        return [pl.BlockSpec(blk, prompt_map), pl.BlockSpec(blk, lambda i: (0, 0, jnp.maximum(i - n_p, 0), 0))]

    row = lambda width: pl.BlockSpec((tm, width), lambda i: (i, 0))
    o_in = [a for g in groups for a in g[:2]]
    l_in = [a for g in groups for a in g[2:]]
    return pl.pallas_call(
        functools.partial(_merge_kernel, n_prompt_tiles=n_p),
        grid=(m // tm,),
        in_specs=[row(d_model)]
                 + [s for g in groups for s in pair(*g[:2])] + [s for g in groups for s in pair(*g[2:])]
                 + [_full_spec(a) for a in weights],
        out_specs=[row(d_model), row(d_model), row(LANES)],
        out_shape=[jax.ShapeDtypeStruct((m, d_model), F32), jax.ShapeDtypeStruct((m, d_model), BF16),
                   jax.ShapeDtypeStruct((m, LANES), F32)],
        scratch_shapes=[pltpu.VMEM((W_C // LANES, tm, LANES), F32), pltpu.VMEM((1, tm, LANES), F32)],
        compiler_params=_params(1),
        name="merge_out_ln_route",
    )(x, *o_in, *l_in, *weights)


def _moe_kernel(te_ref, nt_ref, x_ref, wg_ref, wu_ref, wd_ref, o_ref):
    t = pl.program_id(0)

    @pl.when(t < nt_ref[0])
    def _():
        o_ref[...] = _swiglu(x_ref[...], wg_ref, wu_ref, wd_ref).astype(o_ref.dtype)

    @pl.when(t >= nt_ref[0])
    def _():
        o_ref[...] = jnp.zeros(o_ref.shape, o_ref.dtype)


def _moe(tile_expert, n_tiles_used, xs, wg, wu, wd, tm):
    p, d = xs.shape
    f = wg.shape[2]
    grid_spec = pltpu.PrefetchScalarGridSpec(
        num_scalar_prefetch=2,
        grid=(p // tm,),
        in_specs=[pl.BlockSpec((tm, d), lambda t, te, nt: (t, 0)),
                  pl.BlockSpec((None, d, f), lambda t, te, nt: (te[t], 0, 0), pipeline_mode=pl.Buffered(1)),
                  pl.BlockSpec((None, d, f), lambda t, te, nt: (te[t], 0, 0), pipeline_mode=pl.Buffered(1)),
                  pl.BlockSpec((None, f, d), lambda t, te, nt: (te[t], 0, 0), pipeline_mode=pl.Buffered(1))],
        out_specs=pl.BlockSpec((tm, d), lambda t, te, nt: (t, 0)),
    )
    return pl.pallas_call(
        _moe_kernel,
        grid_spec=grid_spec,
        out_shape=jax.ShapeDtypeStruct((p, d), BF16),
        compiler_params=_params(1),
        name="moe_experts",
    )(tile_expert, n_tiles_used, xs, wg, wu, wd)


def _final_kernel(x_ref, ya_ref, yb_ref, route_ref, g_ref, beta_ref, op_ref, os_ref, *, n_prompt_tiles):
    route = route_ref[...]
    y = route[:, 2:3] * ya_ref[...].astype(F32) + route[:, 3:4] * yb_ref[...].astype(F32)
    out = _layer_norm(ALPHA * x_ref[...] + y, g_ref[...], beta_ref[...])
    i = pl.program_id(0)

    @pl.when(i < n_prompt_tiles)
    def _():
        op_ref[...] = out

    @pl.when(i >= n_prompt_tiles)
    def _():
        os_ref[...] = out


def _final(x, ya, yb, route, g, beta, tm, n_prompt_rows):
    m, d = x.shape
    n_p = n_prompt_rows // tm
    row = lambda arr: pl.BlockSpec((tm, arr.shape[1]), lambda i: (i, 0))
    return pl.pallas_call(
        functools.partial(_final_kernel, n_prompt_tiles=n_p),
        grid=(m // tm,),
        in_specs=[row(x), row(ya), row(yb), row(route), _full_spec(g), _full_spec(beta)],
        out_specs=[pl.BlockSpec((tm, d), lambda i: (jnp.minimum(i, n_p - 1), 0)),
                   pl.BlockSpec((tm, d), lambda i: (jnp.maximum(i - n_p, 0), 0))],
        out_shape=[jax.ShapeDtypeStruct((n_prompt_rows, d), F32),
                   jax.ShapeDtypeStruct((m - n_prompt_rows, d), F32)],
        compiler_params=_params(1),
        name="moe_combine_ln",
    )(x, ya, yb, route, g, beta)


def _rope_table(pos, scale):
    half = DR_A // 2
    inv = ROPE_BASE ** (-jnp.arange(half, dtype=F32) / half)
    ang = pos.astype(F32)[:, None] * inv[None, :]
    cos, sin = jnp.cos(ang), jnp.sin(ang)
    n = pos.shape[0]
    z = lambda w: jnp.zeros((n, w), F32)
    one = jnp.ones((n, DN_A), F32)
    cq = jnp.concatenate([one, cos, cos, z(32)], 1) * scale
    s1q = jnp.concatenate([z(DN_A), -sin, z(16), z(32)], 1) * scale
    s2q = jnp.concatenate([z(DN_A), z(16), sin, z(32)], 1) * scale
    ck = jnp.concatenate([cos, cos, z(96)], 1)
    s1k = jnp.concatenate([-sin, z(112)], 1)
    s2k = jnp.concatenate([z(16), sin, z(96)], 1)
    return jnp.concatenate([cq, s1q, s2q, ck, s1k, s2k], 1)


def _t5_bucket(dist):
    max_exact = N_BUCKETS // 2
    log_ratio = jnp.log(jnp.maximum(dist, 1).astype(F32) / max_exact) / math.log(MAX_DISTANCE / max_exact)
    large = jnp.minimum(max_exact + (log_ratio * (N_BUCKETS - max_exact)).astype(jnp.int32), N_BUCKETS - 1)
    return jnp.where(dist < max_exact, dist, large)


def _group_bias(rel_bias, g):
    d = C_DILATIONS[g]
    dist = d * jnp.arange(C_WINDOWS[g] // d + 1, dtype=jnp.int32)
    onehot = (_t5_bucket(dist)[:, None] == jnp.arange(N_BUCKETS)[None, :]).astype(F32)
    table = rel_bias[:, g * H_C:(g + 1) * H_C]
    return jnp.transpose(jnp.dot(onehot, table, precision=lax.Precision.HIGHEST))


def _prompt_bias(bias):
    n_back = bias.shape[1] - 1
    a = np.arange(C_BLOCK)[:, None]
    c = np.arange(2 * C_BLOCK)[None, :]
    j = a + C_BLOCK - c
    band = (j >= 0) & (j <= n_back)
    full = _select_columns(bias, np.clip(j, 0, n_back).reshape(-1))
    return jnp.where(band[None], full.reshape(-1, C_BLOCK, 2 * C_BLOCK), NEG).reshape(-1, 2 * C_BLOCK)


def _select_columns(bias, cols):
    sel = (jnp.arange(bias.shape[1], dtype=jnp.int32)[:, None] == jnp.asarray(cols, jnp.int32)[None, :])
    return jnp.dot(bias.astype(F32), sel.astype(F32), precision=lax.Precision.HIGHEST)


def _sample_masks(bias, g, n_buf, t_dec):
    d = C_DILATIONS[g]
    n_back = bias.shape[1] - 1
    t = np.arange(t_dec)[:, None]

    def build(key_pos):
        diff = n_buf + t - key_pos[None, :]
        ok = (diff >= 0) & (diff % d == 0) & (diff // d <= n_back)
        full = _select_columns(bias, np.clip(diff // d, 0, n_back).reshape(-1))
        mt = jnp.where(ok[None], full.reshape(H_C, t_dec, -1), NEG)
        return mt.reshape(H_C * t_dec, -1)

    mask_new = build(n_buf + np.arange(LANES))
    col_ok = (np.arange(LANES) < t_dec)[None, :]
    return build(np.arange(n_buf)), jnp.where(col_ok, mask_new, NEG)


def _prep_even(e_w_in, e_q_norm_g, e_kv_norm_g, e_w_uq, e_w_uk, e_w_uv, e_v_ln_g, e_v_ln_b, e_w_s, e_b_s, t_dec):
    o_kr = D_CQ + D_C
    o_uv = o_kr + DR_A
    d_model = e_w_in.shape[0]
    w_in = jnp.concatenate([e_w_in[:, :o_kr], e_w_in[:, o_uv:], e_w_in[:, o_kr:o_uv],
                            jnp.zeros((d_model, LANES - DR_A), F32)], axis=1).astype(BF16)
    w_uq = jnp.pad(e_w_uq, ((0, 0), (0, 0), (0, LANES - DN_A - DR_A))).reshape(D_CQ, H_A * LANES).astype(BF16)
    w_uk_pad = jnp.pad(e_w_uk, ((0, 0), (0, 0), (0, LANES - DN_A))).reshape(D_C, H_A * LANES)
    w_uv = e_w_uv.reshape(D_C, H_A * DV_A)
    place = np.zeros((LANES, H_A, LANES), np.float32)
    for i in range(DR_A):
        place[i, :, DN_A + i] = 1.0
    place = place.reshape(LANES, H_A * LANES)
    w_kv = jnp.concatenate([
        jnp.concatenate([w_uk_pad, w_uv], axis=1),
        jnp.concatenate([jnp.asarray(place), jnp.zeros((LANES, H_A * DV_A), F32)], axis=1)], axis=0).astype(BF16)

    tril = np.tril(np.ones((CHUNK, CHUNK), bool))
    wg_p = jnp.where(tril[None], e_w_s, 0.0)
    small = jnp.where(np.tril(np.ones((t_dec, t_dec), bool))[None], e_w_s[:, :t_dec, :t_dec], 0.0)
    eye = jnp.eye(CHUNK // t_dec, dtype=F32)
    wg_s = jnp.einsum("ab,gts->gatbs", eye, small).reshape(G_B, CHUNK, CHUNK)
    wgate = jnp.stack([wg_p, wg_s]).astype(BF16)
    bg_p = jnp.broadcast_to(e_b_s[:, :, None], (G_B, CHUNK, LANES))
    bg_s = jnp.broadcast_to(jnp.tile(e_b_s[:, :t_dec], (1, CHUNK // t_dec))[:, :, None], (G_B, CHUNK, LANES))
    bgate = jnp.stack([bg_p, bg_s]).astype(F32)

    sel = np.zeros((LANES, Q_LAT_W), np.float32)
    for i in range(DR_A):
        sel[DN_A + i, D_C + i] = 1.0
    wq2 = jnp.broadcast_to(jnp.asarray(sel)[:, None, :], (LANES, H_A, Q_LAT_W))
    uk_t = jnp.transpose(e_w_uk, (2, 1, 0))
    wq2 = wq2.at[:DN_A, :, :D_C].add(uk_t)
    wq2 = wq2.reshape(LANES, H_A * Q_LAT_W).astype(BF16)
    return dict(w_in=w_in, gq=e_q_norm_g.reshape(1, -1), gkv=e_kv_norm_g.reshape(1, -1), w_uq=w_uq, w_kv=w_kv,
                vg=e_v_ln_g.reshape(1, -1), vb=e_v_ln_b.reshape(1, -1), wgate=wgate, bgate=bgate,
                wq2=wq2, wuv=w_uv.astype(BF16))


def _pick_tile(m_prompt, m_sample, cap):
    t = cap
    while m_prompt % t or m_sample % t:
        t //= 2
    assert t >= CHUNK
    return t


def kernel(x_prompt, x_sample, cache_mla_ckv, cache_mla_kpe, cache_dsw_kv0, cache_dsw_kv1, cache_dsw_kv2, page_table, ln_g, ln_b, e_w_in, e_q_norm_g, e_kv_norm_g, e_w_uq, e_w_uk, e_w_uv, e_v_ln_g, e_v_ln_b, e_w_s, e_b_s, e_w_out, ffn_w_gate, ffn_w_up, ffn_w_down, o_w_in, o_w_out, rel_bias, moe_w_router, moe_w_gate, moe_w_up, moe_w_down):
    nb_p, seq, d_model = x_prompt.shape
    nb_s, t_dec, _ = x_sample.shape
    past_len = page_table.shape[1] * PAGE_SIZE
    mp, ms = nb_p * seq, nb_s * t_dec
    m = mp + ms
    tm = _pick_tile(seq, ms, 512)
    tm_p = _pick_tile(seq, seq, 512)
    tm_s = _pick_tile(ms, ms, 512)
    assert CHUNK % t_dec == 0 and seq % CHUNK == 0 and mp % tm_s == 0

    xp0, xs0 = x_prompt.reshape(mp, d_model), x_sample.reshape(ms, d_model)
    ln = lambda l, j: (ln_g[l, j].reshape(1, -1), ln_b[l, j].reshape(1, -1))

    prep = _prep_even(e_w_in[0], e_q_norm_g[0], e_kv_norm_g[0], e_w_uq[0], e_w_uk[0], e_w_uv[0], e_v_ln_g[0],
                      e_v_ln_b[0], e_w_s[0], e_b_s[0], t_dec)
    scale = (DN_A + DR_A) ** -0.5
    pos = jnp.concatenate([jnp.arange(seq, dtype=jnp.int32),
                           past_len + (jnp.arange(tm, dtype=jnp.int32) % t_dec)])
    tab = _rope_table(pos, scale)
    q, k, v, gated, ckv, kpe, vln = _even_in(xp0, xs0, prep, tab, tm, seq // tm)

    att_p = _mla_prompt(q, k, v, nb_p, seq, min(256, seq))
    q_s = jnp.transpose(q[:, mp:].astype(F32), (1, 0, 2)).reshape(nb_s, t_dec, H_A * LANES)
    ckv_s = ckv[mp:].reshape(nb_s, t_dec, D_C)
    kpe_s = kpe[mp:].reshape(nb_s, t_dec, LANES)
    att_s = _mla_sample(page_table, q_s, ckv_s, kpe_s, prep["wq2"], prep["wuv"], cache_mla_ckv,
                        jnp.swapaxes(cache_mla_kpe, 2, 3))
    att_s = jnp.transpose(att_s.reshape(ms, H_A // 2, LANES), (1, 0, 2)).astype(BF16)

    w_out_e = e_w_out[0].astype(BF16)
    n_att = H_A * DV_A
    x1, x1b = _layer0_out(xp0, xs0, att_p, att_s, gated,
                          [w_out_e[:n_att], w_out_e[n_att:], *ln(0, 0), ffn_w_gate[0].astype(BF16),
                           ffn_w_up[0].astype(BF16), ffn_w_down[0].astype(BF16), *ln(0, 1)], tm)

    ng = N_GROUPS_C
    w_in_o = o_w_in[0].reshape(d_model, ng, 3 * W_C).astype(BF16)
    caches = (cache_dsw_kv0, cache_dsw_kv1, cache_dsw_kv2)
    groups, dsw_p, dsw_s = [], [], []
    for g in range(ng):
        w_g = w_in_o[:, g]
        bias = _group_bias(rel_bias, g)
        qd, kd, vd, kv_tail = _odd_in_prompt(x1b, w_g, g, nb_p, seq, tm_p)
        win = min(C_WINDOWS[g], seq)
        dsw_p.append(jnp.transpose(kv_tail.reshape(nb_p, 2, H_C, DH_C, win), (0, 4, 1, 2, 3))[None])
        o_p, lse_p = _dsw_prompt(qd, kd, vd, _prompt_bias(bias), g)

        q_new, kv_new = _odd_in_sample(x1b, w_g, g, mp, ms, tm_s)
        dsw_s.append(kv_new.reshape(1, nb_s, t_dec, 2, H_C, DH_C))
        n_buf = caches[g].shape[2]
        buf_t = jnp.transpose(caches[g][0], (0, 2, 3, 4, 1))
        q_hm = jnp.transpose(q_new.reshape(nb_s, t_dec, H_C, DH_C), (0, 2, 1, 3))
        mask, mask_new = _sample_masks(bias, g, n_buf, t_dec)
        o_s, lse_s = _dsw_sample(q_new, q_hm, kv_new, buf_t, mask, mask_new)
        d = C_DILATIONS[g]
        groups.append((o_p, _residue_major(o_s.astype(BF16), d, tm), lse_p, _residue_major(lse_s, d, tm)))

    expand = np.zeros((LANES, W_C), np.float32)
    for h in range(H_C):
        expand[h, h * DH_C:(h + 1) * DH_C] = 1.0
    wr = jnp.pad(moe_w_router[0], ((0, 0), (0, LANES - N_EXPERTS)))
    wr_hi, wr_lo = _split_bf16(wr)
    weights = [jnp.asarray(expand, BF16), o_w_out[0].astype(BF16), *ln(1, 0), wr_hi, wr_lo]
    x2, x2b, route = _merge(x1, groups, weights, tm, mp, seq // tm)

    tmoe = 512
    idx = route[:, :TOP_K].astype(jnp.int32)
    flat_e = idx.reshape(-1)
    onehot = (flat_e[:, None] == jnp.arange(N_EXPERTS)[None, :]).astype(jnp.int32)
    csum = jnp.cumsum(onehot, axis=0)
    counts = csum[-1]
    rank = jnp.sum(csum * onehot, axis=1) - 1
    padded = ((counts + tmoe - 1) // tmoe) * tmoe
    ends = jnp.cumsum(padded)
    starts = ends - padded
    pos_sorted = jnp.sum(starts[None, :] * onehot, axis=1) + rank
    p_rows = ((TOP_K * m + tmoe - 1) // tmoe + N_EXPERTS) * tmoe
    n_tiles = p_rows // tmoe
    tile_start = jnp.arange(n_tiles, dtype=jnp.int32) * tmoe
    tile_expert = jnp.minimum(jnp.sum(tile_start[:, None] >= ends[None, :], axis=1), N_EXPERTS - 1).astype(jnp.int32)
    n_used = (ends[-1] // tmoe).astype(jnp.int32).reshape(1)
    order = jnp.argsort(flat_e, stable=True).astype(jnp.int32)
    shift = starts - (jnp.cumsum(counts) - counts)
    row_shift = jnp.repeat(jnp.sum(shift[None, :] * (tile_expert[:, None] == jnp.arange(N_EXPERTS)[None, :]), axis=1),
                           tmoe)
    q_idx = jnp.clip(jnp.arange(p_rows, dtype=jnp.int32) - row_shift, 0, TOP_K * m - 1)
    src = order.at[q_idx].get(mode="promise_in_bounds") // TOP_K
    xs = x2b.at[src].get(mode="promise_in_bounds")
    ys = _moe(tile_expert, n_used, xs, moe_w_gate[0].astype(BF16), moe_w_up[0].astype(BF16),
              moe_w_down[0].astype(BF16), tmoe)
    pos2 = pos_sorted.reshape(m, TOP_K)
    ya = ys.at[pos2[:, 0]].get(mode="promise_in_bounds")
    yb = ys.at[pos2[:, 1]].get(mode="promise_in_bounds")
    y_p, y_s = _final(x2, ya, yb, route, *ln(1, 1), tm, mp)

    y_prompt = y_p.reshape(nb_p, seq, d_model)
    y_sample = y_s.reshape(nb_s, t_dec, d_model)
    n_pg = seq // PAGE_SIZE
    new_ckv_prompt = ckv[:mp].reshape(1, nb_p, n_pg, PAGE_SIZE, D_C)
    new_kpe_prompt = kpe[:mp, :DR_A].reshape(1, nb_p, n_pg, PAGE_SIZE, DR_A)
    new_ckv_sample = ckv_s.reshape(1, nb_s, t_dec, D_C)
    new_kpe_sample = kpe_s[:, :, :DR_A].reshape(1, nb_s, t_dec, DR_A)
    new_gate_v_sample = vln[mp:].reshape(1, nb_s, t_dec, D_B)
    return (y_prompt, y_sample, new_ckv_prompt, new_kpe_prompt, new_ckv_sample, new_kpe_sample, new_gate_v_sample,
            dsw_p[0], dsw_p[1], dsw_p[2], dsw_s[0], dsw_s[1], dsw_s[2])
```
